```python
import math
import jax, jax.numpy as jnp
from jax import lax
import numpy as np

D_MODEL = 1024
BATCH = 32
SEQ = 2048
DEPTH = 2

HEAD_DIM = 64
N_DIFF_HEADS = 4
DIFF_V_DIM = 2 * HEAD_DIM
DIFF_QK_WIDTH = N_DIFF_HEADS * 2 * HEAD_DIM
DIFF_WIDTH = N_DIFF_HEADS * DIFF_V_DIM
N_DIL_HEADS = 8
DIL_WIDTH = N_DIL_HEADS * HEAD_DIM
MIX_WIDTH = DIFF_WIDTH + DIL_WIDTH
IN_SPLITS = (DIFF_QK_WIDTH, 2 * DIFF_QK_WIDTH, 2 * DIFF_QK_WIDTH + DIFF_WIDTH,
             2 * DIFF_QK_WIDTH + DIFF_WIDTH + DIL_WIDTH,
             2 * DIFF_QK_WIDTH + DIFF_WIDTH + 2 * DIL_WIDTH)
IN_PROJ_WIDTH = 2 * DIFF_QK_WIDTH + DIFF_WIDTH + 3 * DIL_WIDTH
DILATED_PAIRS = ((128, 1), (512, 4), (2048, 16))
ROPE_THETA = 500000.0
ROPE_DIM = HEAD_DIM // 4
D_FF = 2816
CONV_WIDTH = 3
Q_BLOCK = 128
EPS = 1e-6
NEG_INF = -1e30
MAX_POS_OFFSET = 4096

kernel_name = "hymba_diff_dilated_convffn_adaln"


def rms_norm(x, g):
    xf = x.astype(jnp.float32)
    y = xf * lax.rsqrt(jnp.mean(xf * xf, axis=-1, keepdims=True) + EPS)
    return (y * g.astype(jnp.float32)).astype(x.dtype)


def rope_angles(positions):
    inv_freq = ROPE_THETA ** (-jnp.arange(0, ROPE_DIM, 2, dtype=jnp.float32) / ROPE_DIM)
    ang = positions.astype(jnp.float32)[..., None] * inv_freq
    return jnp.cos(ang), jnp.sin(ang)


def apply_partial_rope(x, cos, sin):
    bshape = cos.shape[:2] + (1,) * (x.ndim - 3) + cos.shape[-1:]
    cos = cos.reshape(bshape).astype(x.dtype)
    sin = sin.reshape(bshape).astype(x.dtype)
    half = ROPE_DIM // 2
    x1, x2 = x[..., :half], x[..., half:ROPE_DIM]
    return jnp.concatenate([x1 * cos - x2 * sin, x2 * cos + x1 * sin, x[..., ROPE_DIM:]], axis=-1)


def differential_attention(q, k, v, lam, lam_init, subln_g):
    B, S = q.shape[:2]
    qh = q.transpose(0, 2, 3, 1, 4)
    kh = k.transpose(0, 2, 3, 1, 4)
    vh = v.transpose(0, 2, 1, 3)
    scale = HEAD_DIM ** -0.5
    outs = []
    for n in range(S // Q_BLOCK):
        q0 = n * Q_BLOCK
        kv_len = q0 + Q_BLOCK
        s = jnp.einsum('bhiqe,bhike->bhiqk', qh[:, :, :, q0:kv_len],
                       kh[:, :, :, :kv_len]).astype(jnp.float32) * scale
        causal = (q0 + jnp.arange(Q_BLOCK))[:, None] >= jnp.arange(kv_len)[None, :]
        p = jax.nn.softmax(jnp.where(causal, s, NEG_INF), axis=-1)
        a = p[:, :, 0] - lam * p[:, :, 1]
        outs.append(jnp.einsum('bhqk,bhkv->bqhv', a.astype(v.dtype), vh[:, :, :kv_len]))
    o = jnp.concatenate(outs, axis=1)
    o = rms_norm(o, subln_g) * (1.0 - lam_init)
    return o.reshape(B, S, DIFF_WIDTH)


def dilated_branch(q, k, v, window, dilation):
    B, S, H, E = q.shape
    r = window // dilation
    L = S // dilation
    nb = -(-L // r)
    Lp = nb * r
    def sub(t):
        return t.reshape(B, L, dilation, H, E)
    qs = jnp.pad(sub(q), ((0, 0), (0, Lp - L), (0, 0), (0, 0), (0, 0))).reshape(B, nb, r, dilation, H, E)
    def ctx(t):
        tp = jnp.pad(sub(t), ((0, 0), (r, Lp - L), (0, 0), (0, 0), (0, 0))).reshape(B, nb + 1, r, dilation, H, E)
        return jnp.concatenate([tp[:, :-1], tp[:, 1:]], axis=2)
    kc, vc = ctx(k), ctx(v)
    s = jnp.einsum('bnqrhe,bnkrhe->bnrhqk', qs, kc).astype(jnp.float32) * (HEAD_DIM ** -0.5)
    qq = jnp.arange(r)[:, None]
    kk = jnp.arange(2 * r)[None, :]
    blk = jnp.arange(nb)[:, None, None]
    valid = (kk >= qq) & (kk <= qq + r) & ((blk - 1) * r + kk >= 0)
    s = jnp.where(valid[None, :, None, None], s, NEG_INF)
    lse = jax.nn.logsumexp(s, axis=-1)
    p = jnp.exp(s - lse[..., None])
    o = jnp.einsum('bnrhqk,bnkrhe->bnqrhe', p.astype(v.dtype), vc)
    o = o.reshape(B, Lp, dilation, H, E)[:, :L].reshape(B, S, H, E)
    lse = lse.transpose(0, 1, 4, 2, 3).reshape(B, Lp, dilation, H)[:, :L].reshape(B, S, H)
    return o, lse


def dilated_attention(q, k, v):
    outs, lses = [], []
    for window, dilation in DILATED_PAIRS:
        o, lse = dilated_branch(q, k, v, window, dilation)
        outs.append(o)
        lses.append(lse)
    w = jax.nn.softmax(jnp.stack(lses, axis=0), axis=0)
    o = jnp.sum(w[..., None] * jnp.stack(outs, axis=0).astype(jnp.float32), axis=0)
    return o.astype(q.dtype)


def hybrid_mixer(h, cos, sin, lam_init, w_in, w_out, diff_q_g, diff_k_g, lam_q1, lam_k1, lam_q2, lam_k2,
                 diff_subln_g, dil_q_g, dil_k_g, dil_out_g):
    B, S, _ = h.shape
    proj = h @ w_in
    dq, dk, dv, lq, lk, lv = jnp.split(proj, IN_SPLITS, axis=-1)
    dq = apply_partial_rope(rms_norm(dq.reshape(B, S, N_DIFF_HEADS, 2, HEAD_DIM), diff_q_g), cos, sin)
    dk = apply_partial_rope(rms_norm(dk.reshape(B, S, N_DIFF_HEADS, 2, HEAD_DIM), diff_k_g), cos, sin)
    dv = dv.reshape(B, S, N_DIFF_HEADS, DIFF_V_DIM)
    lam = (jnp.exp(jnp.sum(lam_q1.astype(jnp.float32) * lam_k1.astype(jnp.float32)))
           - jnp.exp(jnp.sum(lam_q2.astype(jnp.float32) * lam_k2.astype(jnp.float32))) + lam_init)
    out_a = differential_attention(dq, dk, dv, lam, lam_init, diff_subln_g)
    lq = apply_partial_rope(rms_norm(lq.reshape(B, S, N_DIL_HEADS, HEAD_DIM), dil_q_g), cos, sin)
    lk = apply_partial_rope(rms_norm(lk.reshape(B, S, N_DIL_HEADS, HEAD_DIM), dil_k_g), cos, sin)
    lv = lv.reshape(B, S, N_DIL_HEADS, HEAD_DIM)
    out_b = rms_norm(dilated_attention(lq, lk, lv), dil_out_g).reshape(B, S, DIL_WIDTH)
    return jnp.concatenate([out_a, out_b], axis=-1) @ w_out


def conv_ffn(h, w_up, conv_w, conv_b, w_down):
    u = h @ w_up
    u = lax.conv_general_dilated(u, conv_w[:, None, :], window_strides=(1,),
                                 padding=[(CONV_WIDTH - 1, 0)],
                                 dimension_numbers=('NWC', 'WIO', 'NWC'),
                                 feature_group_count=2 * D_FF) + conv_b
    g, val = jnp.split(u, 2, axis=-1)
    return (jax.nn.silu(g) * val) @ w_down


def setup_inputs(seed: int = 0) -> dict:
    key = jax.random.key(seed)
    ks = jax.random.split(key, 24)
    f32 = jnp.float32
    def nrm(k, shape, scale):
        return jax.random.normal(k, shape, f32) * scale
    def gain(k, shape):
        return 1.0 + 0.02 * jax.random.normal(k, shape, f32)
    start = jax.random.randint(ks[2], (BATCH, 1), 0, MAX_POS_OFFSET, dtype=jnp.int32)
    positions = start + jnp.arange(SEQ, dtype=jnp.int32)[None, :]
    return {
        "x": nrm(ks[0], (BATCH, SEQ, D_MODEL), 1.0),
        "c": nrm(ks[1], (BATCH, D_MODEL), 1.0),
        "positions": positions,
        "g_mix": gain(ks[3], (DEPTH, D_MODEL)),
        "g_ffn": gain(ks[4], (DEPTH, D_MODEL)),
        "w_ada": nrm(ks[5], (DEPTH, D_MODEL, 6 * D_MODEL), 0.5 * D_MODEL ** -0.5),
        "b_ada": nrm(ks[6], (DEPTH, 6 * D_MODEL), 0.01),
        "w_in": nrm(ks[7], (DEPTH, D_MODEL, IN_PROJ_WIDTH), D_MODEL ** -0.5),
        "w_out": nrm(ks[8], (DEPTH, MIX_WIDTH, D_MODEL), MIX_WIDTH ** -0.5),
        "diff_q_g": gain(ks[9], (DEPTH, HEAD_DIM)),
        "diff_k_g": gain(ks[10], (DEPTH, HEAD_DIM)),
        "lam_q1": nrm(ks[11], (DEPTH, HEAD_DIM), 0.1),
        "lam_k1": nrm(ks[12], (DEPTH, HEAD_DIM), 0.1),
        "lam_q2": nrm(ks[13], (DEPTH, HEAD_DIM), 0.1),
        "lam_k2": nrm(ks[14], (DEPTH, HEAD_DIM), 0.1),
        "diff_subln_g": gain(ks[15], (DEPTH, DIFF_V_DIM)),
        "dil_q_g": gain(ks[16], (DEPTH, HEAD_DIM)),
        "dil_k_g": gain(ks[17], (DEPTH, HEAD_DIM)),
        "dil_out_g": gain(ks[18], (DEPTH, HEAD_DIM)),
        "w_up": nrm(ks[19], (DEPTH, D_MODEL, 2 * D_FF), D_MODEL ** -0.5),
        "conv_w": nrm(ks[20], (DEPTH, CONV_WIDTH, 2 * D_FF), CONV_WIDTH ** -0.5),
        "conv_b": nrm(ks[21], (DEPTH, 2 * D_FF), 0.01),
        "w_down": nrm(ks[22], (DEPTH, D_FF, D_MODEL), D_FF ** -0.5),
    }


def reference(x, c, positions, g_mix, g_ffn, w_ada, b_ada, w_in, w_out, diff_q_g, diff_k_g,
              lam_q1, lam_k1, lam_q2, lam_k2, diff_subln_g, dil_q_g, dil_k_g, dil_out_g,
              w_up, conv_w, conv_b, w_down):
    cos, sin = rope_angles(positions)
    cond = jax.nn.silu(c)
    for l in range(DEPTH):
        lam_init = 0.8 - 0.6 * math.exp(-0.3 * l)
        mod = (cond @ w_ada[l] + b_ada[l])[:, None, :]
        sh_a, sc_a, gt_a, sh_f, sc_f, gt_f = jnp.split(mod, 6, axis=-1)
        h = rms_norm(x, g_mix[l]) * (1.0 + sc_a) + sh_a
        x = x + gt_a * hybrid_mixer(h, cos, sin, lam_init, w_in[l], w_out[l], diff_q_g[l], diff_k_g[l],
                                    lam_q1[l], lam_k1[l], lam_q2[l], lam_k2[l], diff_subln_g[l],
                                    dil_q_g[l], dil_k_g[l], dil_out_g[l])
        h = rms_norm(x, g_ffn[l]) * (1.0 + sc_f) + sh_f
        x = x + gt_f * conv_ffn(h, w_up[l], conv_w[l], conv_b[l], w_down[l])
    return x
```

```python
import functools
import math

import jax
import jax.numpy as jnp
import numpy as np
from jax import lax
from jax.experimental import pallas as pl
from jax.experimental.pallas import tpu as pltpu

HEAD_DIM = 64
N_DIFF_HEADS = 4
DIFF_V_DIM = 2 * HEAD_DIM
DIFF_WIDTH = N_DIFF_HEADS * DIFF_V_DIM
N_DIL_HEADS = 8
DIL_WIDTH = N_DIL_HEADS * HEAD_DIM
DILATED_PAIRS = ((128, 1), (512, 4), (2048, 16))
ROPE_THETA = 500000.0
ROPE_DIM = HEAD_DIM // 4
ROPE_HALF = ROPE_DIM // 2
CONV_WIDTH = 3
EPS = 1e-6
NEG_INF = -1e30
LOG2E = 1.4426950408889634

LANES = 128
MXU_DIM = 256
TOKEN_TILE = 256
DIL_BLOCK = 128
FF_CHUNK = 256
VMEM_LIMIT = 56 * 1024 * 1024

F32 = jnp.float32
BF16 = jnp.bfloat16


def _dot(a, b):
    return jnp.dot(a, b, preferred_element_type=F32)


def _dot_nt(a, b):
    return lax.dot_general(a, b, (((1,), (1,)), ((), ())), preferred_element_type=F32)


def _split_bf16(x):
    hi = x.astype(BF16)
    lo = (x - hi.astype(F32)).astype(BF16)
    return hi, lo


def _params(sem, vmem=VMEM_LIMIT):
    return pltpu.CompilerParams(dimension_semantics=sem, vmem_limit_bytes=vmem)


def _mod_kernel(c_ref, w_ref, b_ref, q1_ref, k1_ref, q2_ref, k2_ref, li_ref, mod_ref, lam_ref):
    c = c_ref[...]
    cond = c * (1.0 / (1.0 + jnp.exp(-c)))
    ch, cl = _split_bf16(cond)
    wh, wl = _split_bf16(w_ref[0])
    mod_ref[0] = _dot(ch, wh) + (_dot(ch, wl) + _dot(cl, wh)) + b_ref[0]
    s1 = jnp.sum(q1_ref[0] * k1_ref[0], axis=-1, keepdims=True)
    s2 = jnp.sum(q2_ref[0] * k2_ref[0], axis=-1, keepdims=True)
    lam_ref[0] = (jnp.exp(s1) - jnp.exp(s2)) + li_ref[0]


def _mod_call(c, w_ada, b_ada, lam_q1, lam_k1, lam_q2, lam_k2, lam_init):
    depth, d, d6 = w_ada.shape
    b = c.shape[0]
    nj = d6 // d
    vec = lambda a: a.reshape(depth, 1, HEAD_DIM)
    vspec = pl.BlockSpec((1, 1, HEAD_DIM), lambda l, j: (l, 0, 0))
    return pl.pallas_call(
        _mod_kernel,
        grid=(depth, nj),
        in_specs=[
            pl.BlockSpec((b, d), lambda l, j: (0, 0)),
            pl.BlockSpec((1, d, d), lambda l, j: (l, 0, j)),
            pl.BlockSpec((1, 1, d), lambda l, j: (l, 0, j)),
            vspec, vspec, vspec, vspec,
            pl.BlockSpec((1, 1, LANES), lambda l, j: (l, 0, 0)),
        ],
        out_specs=[
            pl.BlockSpec((1, b, d), lambda l, j: (l, 0, j)),
            pl.BlockSpec((1, 1, LANES), lambda l, j: (l, 0, 0)),
        ],
        out_shape=[
            jax.ShapeDtypeStruct((depth, b, d6), F32),
            jax.ShapeDtypeStruct((depth, 1, LANES), F32),
        ],
        compiler_params=_params(("arbitrary", "arbitrary")),
        name="adaln_mod",
    )(c, w_ada, b_ada.reshape(depth, 1, d6), vec(lam_q1), vec(lam_k1), vec(lam_q2), vec(lam_k2), lam_init)


def _rope_kernel(pos_ref, invf_ref, c_ref, s_ref):
    ang = pos_ref[0].astype(F32) * invf_ref[...]
    d = lax.broadcasted_iota(jnp.int32, ang.shape, 1) % HEAD_DIM
    cos = jnp.cos(ang)
    sin = jnp.sin(ang)
    c_ref[0] = jnp.where(d < ROPE_DIM, cos, 1.0)
    s_ref[0] = jnp.where(d < ROPE_HALF, -sin, jnp.where(d < ROPE_DIM, sin, 0.0))


def _rope_call(positions):
    b, s = positions.shape
    ts = 512
    inv_freq = ROPE_THETA ** (-jnp.arange(0, ROPE_DIM, 2, dtype=F32) / ROPE_DIM)
    invf = jnp.tile(inv_freq, LANES // ROPE_HALF).reshape(1, LANES)
    out = jax.ShapeDtypeStruct((b, s, LANES), F32)
    return pl.pallas_call(
        _rope_kernel,
        grid=(b, s // ts),
        in_specs=[
            pl.BlockSpec((1, ts, 1), lambda i, j: (i, j, 0)),
            pl.BlockSpec((1, LANES), lambda i, j: (0, 0)),
        ],
        out_specs=[pl.BlockSpec((1, ts, LANES), lambda i, j: (i, j, 0))] * 2,
        out_shape=[out, out],
        compiler_params=_params(("parallel", "parallel")),
        name="rope_tables",
    )(positions.reshape(b, s, 1), invf)


def _modulated_norm(x, g, shift, scale):
    ms = jnp.mean(x * x, axis=-1, keepdims=True)
    return (x * lax.rsqrt(ms + EPS) * g) * (1.0 + scale) + shift


def _group_mean_sq(y, r_ref):
    return _dot((y * y).astype(BF16), r_ref[...])


def _in_proj_kernel(x_ref, mod_ref, g_ref, wqk_ref, wdvt_ref, wlv_ref, r_ref, gqk_ref, rc_ref, rs_ref,
                    dq_ref, dk_ref, lq_ref, lk_ref, lv_ref, dvt_ref):
    d = x_ref.shape[-1]
    mod = mod_ref[0]
    h = _modulated_norm(x_ref[0], g_ref[...], mod[:, 0:d], mod[:, d:2 * d]).astype(BF16)
    rope_c = rc_ref[0]
    rope_s = rs_ref[0]
    first_half = (lax.broadcasted_iota(jnp.int32, rope_c.shape, 1) % HEAD_DIM) < ROPE_HALF
    pqk = _dot(h, wqk_ref[...])
    outs = (dq_ref, dk_ref, lq_ref, lk_ref)
    per_out = DIFF_WIDTH // MXU_DIM
    for c in range(pqk.shape[1] // MXU_DIM):
        xc = pqk[:, c * MXU_DIM:(c + 1) * MXU_DIM]
        r = lax.rsqrt(_group_mean_sq(xc, r_ref) + EPS)
        y = xc * r * gqk_ref[:, c * MXU_DIM:(c + 1) * MXU_DIM]
        o_ref = outs[c // per_out]
        for t in range(MXU_DIM // LANES):
            yt = y[:, t * LANES:(t + 1) * LANES]
            partner = jnp.where(first_half, pltpu.roll(yt, LANES - ROPE_HALF, 1), pltpu.roll(yt, ROPE_HALF, 1))
            col = (c % per_out) * MXU_DIM + t * LANES
            o_ref[0, :, col:col + LANES] = (yt * rope_c + partner * rope_s).astype(BF16)
    lv_ref[0] = _dot(h, wlv_ref[...]).astype(BF16)
    dvt_ref[0, 0] = _dot_nt(wdvt_ref[...], h).astype(BF16)


def _in_proj_call(x, mod_l, g, wqk, wdvt, wlv, rmat, gqk, rope_c, rope_s):
    b, s, d = x.shape
    tm = TOKEN_TILE
    nt = s // tm
    const = lambda shape: pl.BlockSpec(shape, lambda i, j: (0,) * len(shape))
    tok = lambda w: pl.BlockSpec((1, tm, w), lambda i, j: (i, j, 0))
    o512 = jax.ShapeDtypeStruct((b, s, DIFF_WIDTH), BF16)
    return pl.pallas_call(
        _in_proj_kernel,
        grid=(b, nt),
        in_specs=[
            tok(d),
            pl.BlockSpec((1, 1, mod_l.shape[-1]), lambda i, j: (i, 0, 0)),
            const((1, d)),
            const(wqk.shape), const(wdvt.shape), const(wlv.shape), const(rmat.shape), const(gqk.shape),
            tok(LANES), tok(LANES),
        ],
        out_specs=[tok(DIFF_WIDTH)] * 5 + [pl.BlockSpec((1, 1, DIFF_WIDTH, tm), lambda i, j: (i, j, 0, 0))],
        out_shape=[o512] * 5 + [jax.ShapeDtypeStruct((b, nt, DIFF_WIDTH, tm), BF16)],
        compiler_params=_params(("parallel", "parallel")),
        name="in_proj",
    )(x, mod_l, g, wqk, wdvt, wlv, rmat, gqk, rope_c, rope_s)


def _diff_attn_kernel(q_ref, k_ref, vt_ref, lam_ref, g_ref, o_ref):
    tq = q_ref.shape[1]
    tk = vt_ref.shape[-1]
    qi = pl.program_id(2)
    q = q_ref[0]
    lane = lax.broadcasted_iota(jnp.int32, q.shape, 1)
    zero = jnp.zeros_like(q)
    qs = (jnp.where(lane < HEAD_DIM, q, zero), jnp.where(lane >= HEAD_DIM, q, zero))

    def step(j, carry, masked):
        k = k_ref[0, pl.ds(pl.multiple_of(j * tk, tk), tk), :]
        vt = vt_ref[0, j]
        new = []
        for i in range(2):
            m, l, acc = carry[3 * i:3 * i + 3]
            s = _dot_nt(k, qs[i])
            if masked:
                key = lax.broadcasted_iota(jnp.int32, s.shape, 0)
                qry = lax.broadcasted_iota(jnp.int32, s.shape, 1)
                s = jnp.where(key <= qry, s, NEG_INF)
            m_new = jnp.maximum(m, jnp.max(s, axis=0, keepdims=True))
            alpha = jnp.exp2(m - m_new)
            p = jnp.exp2(s - m_new)
            l = alpha * l + jnp.sum(p, axis=0, keepdims=True)
            acc = alpha * acc + _dot(vt, p.astype(BF16))
            new += [m_new, l, acc]
        return tuple(new)

    m0 = jnp.full((1, tq), NEG_INF, F32)
    l0 = jnp.zeros((1, tq), F32)
    a0 = jnp.zeros((vt_ref.shape[2], tq), F32)
    carry = lax.fori_loop(0, qi, lambda j, c: step(j, c, False), (m0, l0, a0, m0, l0, a0))
    m1, l1, acc1, m2, l2, acc2 = step(qi, carry, True)
    lam = lam_ref[0:1, 0:1]
    ot = acc1 * (1.0 / l1) - acc2 * (lam / l2)
    o = ot.T
    ms = jnp.mean(o * o, axis=-1, keepdims=True)
    o_ref[0] = (o * lax.rsqrt(ms + EPS) * g_ref[...]).astype(o_ref.dtype)


def _diff_attn_call(dq, dk, dvt, lam_l, g_sub):
    b, s, _ = dq.shape
    tq = TOKEN_TILE
    nkv = dvt.shape[1]
    return pl.pallas_call(
        _diff_attn_kernel,
        grid=(b, N_DIFF_HEADS, s // tq),
        in_specs=[
            pl.BlockSpec((1, tq, DIFF_V_DIM), lambda i, h, j: (i, j, h)),
            pl.BlockSpec((1, s, DIFF_V_DIM), lambda i, h, j: (i, 0, h)),
            pl.BlockSpec((1, nkv, DIFF_V_DIM, dvt.shape[-1]), lambda i, h, j: (i, 0, h, 0)),
            pl.BlockSpec((1, LANES), lambda i, h, j: (0, 0)),
            pl.BlockSpec((1, DIFF_V_DIM), lambda i, h, j: (0, 0)),
        ],
        out_specs=pl.BlockSpec((1, tq, DIFF_V_DIM), lambda i, h, j: (i, j, h)),
        out_shape=jax.ShapeDtypeStruct((b, s, DIFF_WIDTH), BF16),
        compiler_params=_params(("parallel", "parallel", "arbitrary")),
        name="diff_attn",
    )(dq, dk, dvt, lam_l, g_sub)


def _dil_attn_kernel(q_ref, k_ref, v_ref, o_ref, st_ref):
    length = q_ref.shape[1]
    nb = length // DIL_BLOCK
    ctx = DIL_BLOCK if nb == 1 else 2 * DIL_BLOCK
    lane = lax.broadcasted_iota(jnp.int32, (DIL_BLOCK, LANES), 1)
    rel = (lax.broadcasted_iota(jnp.int32, (DIL_BLOCK, ctx), 0)
           - lax.broadcasted_iota(jnp.int32, (DIL_BLOCK, ctx), 1))

    def body(n, _):
        start = jnp.maximum(n - 1, 0) * DIL_BLOCK
        off = n * DIL_BLOCK - start
        q0 = pl.multiple_of(n * DIL_BLOCK, DIL_BLOCK)
        k0 = pl.multiple_of(start, DIL_BLOCK)
        delta = rel + off
        valid = (delta >= 0) & (delta <= DIL_BLOCK)
        stats = jnp.zeros((DIL_BLOCK, LANES), F32)
        for hp in range(N_DIL_HEADS * HEAD_DIM // LANES):
            cols = slice(hp * LANES, (hp + 1) * LANES)
            q = q_ref[0, pl.ds(q0, DIL_BLOCK), cols]
            k = k_ref[0, pl.ds(k0, ctx), cols]
            v = v_ref[0, pl.ds(k0, ctx), cols]
            outs = []
            for hh in range(LANES // HEAD_DIM):
                mine = (lane >= hh * HEAD_DIM) & (lane < (hh + 1) * HEAD_DIM)
                s = _dot_nt(jnp.where(mine, q, jnp.zeros_like(q)), k)
                s = jnp.where(valid, s, NEG_INF)
                m = jnp.max(s, axis=-1, keepdims=True)
                p = jnp.exp2(s - m)
                l = jnp.sum(p, axis=-1, keepdims=True)
                outs.append(_dot(p.astype(BF16), v) * (1.0 / l))
                stats = jnp.where(lane == hp * (LANES // HEAD_DIM) + hh, m + jnp.log2(l), stats)
            o_ref[0, pl.ds(q0, DIL_BLOCK), cols] = jnp.where(lane < HEAD_DIM, outs[0], outs[1]).astype(o_ref.dtype)
        st_ref[0, pl.ds(q0, DIL_BLOCK), :] = stats
        return 0

    lax.fori_loop(0, nb, body, 0)


def _dil_attn_call(lq, lk, lv, dilation):
    b, s, w = lq.shape
    length = s // dilation
    view = lambda a: a.reshape(b, length, dilation * a.shape[-1])
    spec = lambda width: pl.BlockSpec((1, length, width), lambda i, r: (i, 0, r))
    o, st = pl.pallas_call(
        _dil_attn_kernel,
        grid=(b, dilation),
        in_specs=[spec(w)] * 3,
        out_specs=[spec(w), spec(LANES)],
        out_shape=[
            jax.ShapeDtypeStruct((b, length, dilation * w), BF16),
            jax.ShapeDtypeStruct((b, length, dilation * LANES), F32),
        ],
        compiler_params=_params(("parallel", "parallel")),
        name=f"dil_attn_d{dilation}",
    )(view(lq), view(lk), view(lv))
    return o.reshape(b, s, w), st.reshape(b, s, LANES)


def _out_proj_kernel(x_ref, mod_ref, oa_ref, o1_ref, o2_ref, o3_ref, s1_ref, s2_ref, s3_ref,
                     e_ref, r_ref, gd_ref, wa_ref, wb_ref, xo_ref):
    d = x_ref.shape[-1]
    sts = (s1_ref[0], s2_ref[0], s3_ref[0])
    mx = jnp.maximum(jnp.maximum(sts[0], sts[1]), sts[2])
    es = [jnp.exp2(st - mx) for st in sts]
    inv = 1.0 / (es[0] + es[1] + es[2])
    ob = None
    for e, o_ref in zip(es, (o1_ref, o2_ref, o3_ref)):
        hi, lo = _split_bf16(e * inv)
        w = _dot(hi, e_ref[...]) + _dot(lo, e_ref[...])
        term = w * o_ref[0].astype(F32)
        ob = term if ob is None else ob + term
    parts = []
    for c in range(DIL_WIDTH // MXU_DIM):
        oc = ob[:, c * MXU_DIM:(c + 1) * MXU_DIM]
        parts.append(oc * lax.rsqrt(_group_mean_sq(oc, r_ref) + EPS))
    ob = (jnp.concatenate(parts, axis=-1) * gd_ref[...]).astype(BF16)
    y = _dot(oa_ref[0], wa_ref[...]) + _dot(ob, wb_ref[...])
    xo_ref[0] = x_ref[0] + mod_ref[0][:, 2 * d:3 * d] * y


def _out_proj_call(x, mod_l, oa, o_dil, st_dil, emat, rmat, gd, wa, wb):
    b, s, d = x.shape
    tm = TOKEN_TILE
    const = lambda shape: pl.BlockSpec(shape, lambda i, j: (0,) * len(shape))
    tok = lambda w: pl.BlockSpec((1, tm, w), lambda i, j: (i, j, 0))
    return pl.pallas_call(
        _out_proj_kernel,
        grid=(b, s // tm),
        in_specs=[tok(d), pl.BlockSpec((1, 1, mod_l.shape[-1]), lambda i, j: (i, 0, 0))]
        + [tok(DIFF_WIDTH)] * 4 + [tok(LANES)] * 3
        + [const(emat.shape), const(rmat.shape), const(gd.shape), const(wa.shape), const(wb.shape)],
        out_specs=tok(d),
        out_shape=jax.ShapeDtypeStruct(x.shape, x.dtype),
        compiler_params=_params(("parallel", "parallel")),
        name="out_proj",
    )(x, mod_l, oa, *o_dil, *st_dil, emat, rmat, gd, wa, wb)


def _ffn_kernel(x_ref, mod_ref, g_ref, wg_ref, wv_ref, cw_ref, cb_ref, wd_ref, xo_ref, tail_ref, y_ref):
    d = x_ref.shape[-1]
    tm = x_ref.shape[1]
    nchunk = wg_ref.shape[0]
    halo = tail_ref.shape[2]

    @pl.when(pl.program_id(1) == 0)
    def _():
        tail_ref[...] = jnp.zeros_like(tail_ref)

    x = x_ref[0]
    mod = mod_ref[0]
    h = _modulated_norm(x, g_ref[...], mod[:, 3 * d:4 * d], mod[:, 4 * d:5 * d]).astype(BF16)
    y_ref[...] = jnp.zeros_like(y_ref)

    def chunk(c, _):
        acts = []
        for part, w_ref in enumerate((wg_ref, wv_ref)):
            u = _dot(h, w_ref[c])
            ext = jnp.concatenate([tail_ref[c, part], u], axis=0)
            tail_ref[c, part] = u[tm - halo:, :]
            cw = cw_ref[c, part]
            conv = cb_ref[c, part]
            for tap in range(CONV_WIDTH):
                lag = CONV_WIDTH - 1 - tap
                conv = conv + ext[halo - lag:halo - lag + tm, :] * cw[tap:tap + 1, :]
            acts.append(conv)
        gate, val = acts
        act = (gate * (1.0 / (1.0 + jnp.exp(-gate))) * val).astype(BF16)
        y_ref[...] += _dot(act, wd_ref[c])
        return 0

    lax.fori_loop(0, nchunk, chunk, 0)
    xo_ref[0] = x + mod[:, 5 * d:6 * d] * y_ref[...]


def _ffn_call(x, mod_l, g, wg, wv, cw, cb, wd):
    b, s, d = x.shape
    tm = TOKEN_TILE
    halo = 8
    const = lambda shape: pl.BlockSpec(shape, lambda i, j: (0,) * len(shape))
    tok = pl.BlockSpec((1, tm, d), lambda i, j: (i, j, 0))
    return pl.pallas_call(
        _ffn_kernel,
        grid=(b, s // tm),
        in_specs=[tok, pl.BlockSpec((1, 1, mod_l.shape[-1]), lambda i, j: (i, 0, 0)), const((1, d)),
                  const(wg.shape), const(wv.shape), const(cw.shape), const(cb.shape), const(wd.shape)],
        out_specs=tok,
        out_shape=jax.ShapeDtypeStruct(x.shape, x.dtype),
        scratch_shapes=[
            pltpu.VMEM((wg.shape[0], 2, halo, FF_CHUNK), F32),
            pltpu.VMEM((tm, d), F32),
        ],
        compiler_params=_params(("parallel", "arbitrary")),
        name="conv_ffn",
    )(x, mod_l, g, wg, wv, cw, cb, wd)


def _group_mean_matrix():
    idx = np.arange(MXU_DIM) // HEAD_DIM
    return jnp.asarray((idx[:, None] == idx[None, :]).astype(np.float32) / HEAD_DIM, dtype=BF16)


def _head_expand_matrix():
    rows = np.arange(LANES)[:, None]
    cols = np.arange(DIL_WIDTH)[None, :] // HEAD_DIM
    return jnp.asarray((rows == cols).astype(np.float32), dtype=BF16)


def kernel(x, c, positions, g_mix, g_ffn, w_ada, b_ada, w_in, w_out, diff_q_g, diff_k_g, lam_q1, lam_k1, lam_q2, lam_k2, diff_subln_g, dil_q_g, dil_k_g, dil_out_g, w_up, conv_w, conv_b, w_down):
    depth, d, _ = w_in.shape
    b = x.shape[0]
    d_ff = w_down.shape[1]
    nchunk = d_ff // FF_CHUNK
    qk_scale = HEAD_DIM ** -0.5 * LOG2E

    lam_init = np.array([0.8 - 0.6 * math.exp(-0.3 * l) for l in range(depth)], np.float32)
    lam_init_tile = jnp.asarray(np.broadcast_to(lam_init[:, None, None], (depth, 1, LANES)))
    mod, lam = _mod_call(c, w_ada, b_ada, lam_q1, lam_k1, lam_q2, lam_k2, lam_init_tile)
    rope_c, rope_s = _rope_call(positions)
    rmat = _group_mean_matrix()
    emat = _head_expand_matrix()

    for l in range(depth):
        dq_w, dk_w, dv_w, lq_w, lk_w, lv_w = jnp.split(
            w_in[l], np.cumsum([DIFF_WIDTH, DIFF_WIDTH, DIFF_WIDTH, DIL_WIDTH, DIL_WIDTH])[:5].tolist(), axis=1)
        wqk = jnp.concatenate([dq_w, dk_w, lq_w, lk_w], axis=1).astype(BF16)
        wdvt = dv_w.T.astype(BF16)
        wlv = lv_w.astype(BF16)
        reps = DIFF_WIDTH // HEAD_DIM
        gqk = jnp.concatenate([jnp.tile(diff_q_g[l], reps) * qk_scale, jnp.tile(diff_k_g[l], reps),
                               jnp.tile(dil_q_g[l], reps) * qk_scale, jnp.tile(dil_k_g[l], reps)]).reshape(1, -1)
        mod_l = mod[l].reshape(b, 1, -1)

        dq, dk, lq, lk, lv, dvt = _in_proj_call(x, mod_l, g_mix[l].reshape(1, d), wqk, wdvt, wlv, rmat, gqk,
                                                rope_c, rope_s)
        g_sub = (diff_subln_g[l] * (1.0 - float(lam_init[l]))).reshape(1, DIFF_V_DIM)
        oa = _diff_attn_call(dq, dk, dvt, lam[l], g_sub)
        o_dil, st_dil = zip(*[_dil_attn_call(lq, lk, lv, dil) for _, dil in DILATED_PAIRS])
        gd = jnp.tile(dil_out_g[l], N_DIL_HEADS).reshape(1, DIL_WIDTH)
        wo = w_out[l].astype(BF16)
        x = _out_proj_call(x, mod_l, oa, o_dil, st_dil, emat, rmat, gd, wo[:DIFF_WIDTH], wo[DIFF_WIDTH:])

        def chunked_cols(a):
            return a.reshape(a.shape[0], 2, nchunk, FF_CHUNK).transpose(2, 1, 0, 3)
        wup = chunked_cols(w_up[l].astype(BF16))
        cw = chunked_cols(conv_w[l])
        cb = chunked_cols(conv_b[l].reshape(1, -1))
        wd = w_down[l].astype(BF16).reshape(nchunk, FF_CHUNK, d)
        x = _ffn_call(x, mod_l, g_ffn[l].reshape(1, d), wup[:, 0], wup[:, 1], cw, cb, wd)
    return x
```

```python
import functools
import math

import jax
import jax.numpy as jnp
import numpy as np
from jax import lax
from jax.experimental import pallas as pl
from jax.experimental.pallas import tpu as pltpu

HEAD_DIM = 64
N_DIFF_HEADS = 4
DIFF_V_DIM = 2 * HEAD_DIM
DIFF_WIDTH = N_DIFF_HEADS * DIFF_V_DIM
N_DIL_HEADS = 8
DIL_WIDTH = N_DIL_HEADS * HEAD_DIM
DILATED_PAIRS = ((128, 1), (512, 4), (2048, 16))
ROPE_THETA = 500000.0
ROPE_DIM = HEAD_DIM // 4
ROPE_HALF = ROPE_DIM // 2
CONV_WIDTH = 3
EPS = 1e-6
NEG_INF = -1e30
LOG2E = 1.4426950408889634
QK_SCALE = HEAD_DIM ** -0.5 * LOG2E

LANES = 128
SUBLANES = 8
MXU_DIM = 256
TOKEN_TILE = 256
DIL_BLOCK = 128
FF_CHUNK = 256
FF_DOWN_GROUP = 6
SAFE_LOGIT = 64.0
VMEM_LIMIT = 56 * 1024 * 1024

F32 = jnp.float32
BF16 = jnp.bfloat16


def _dot(a, b):
    return jnp.dot(a, b, preferred_element_type=F32)


def _dot_nt(a, b):
    return lax.dot_general(a, b, (((1,), (1,)), ((), ())), preferred_element_type=F32)


def _split_bf16(x):
    hi = x.astype(BF16)
    lo = (x - hi.astype(F32)).astype(BF16)
    return hi, lo


def _params(sem, vmem=VMEM_LIMIT):
    return pltpu.CompilerParams(dimension_semantics=sem, vmem_limit_bytes=vmem)


def _const_spec(shape, single_buffer=False):
    mode = pl.Buffered(1) if single_buffer else None
    return pl.BlockSpec(shape, lambda *_: (0,) * len(shape), pipeline_mode=mode)


_SMEM_SPEC = pl.BlockSpec(memory_space=pltpu.SMEM)


def _mod_kernel(c_ref, w_ref, b_ref, q1_ref, k1_ref, q2_ref, k2_ref, li_ref, mod_ref, lam_ref):
    c = c_ref[...]
    cond = c * (1.0 / (1.0 + jnp.exp(-c)))
    ch, cl = _split_bf16(cond)
    wh, wl = _split_bf16(w_ref[0])
    mod_ref[0] = _dot(ch, wh) + (_dot(ch, wl) + _dot(cl, wh)) + b_ref[0]
    s1 = jnp.sum(q1_ref[0] * k1_ref[0], axis=-1, keepdims=True)
    s2 = jnp.sum(q2_ref[0] * k2_ref[0], axis=-1, keepdims=True)
    lam_ref[0] = (jnp.exp(s1) - jnp.exp(s2)) + li_ref[0]


def _mod_call(c, w_ada, b_ada, lam_q1, lam_k1, lam_q2, lam_k2, lam_init):
    depth, d, d6 = w_ada.shape
    b = c.shape[0]
    nj = d6 // d
    vec = lambda a: a.reshape(depth, 1, HEAD_DIM)
    vspec = pl.BlockSpec((1, 1, HEAD_DIM), lambda l, j: (l, 0, 0))
    return pl.pallas_call(
        _mod_kernel,
        grid=(depth, nj),
        in_specs=[
            pl.BlockSpec((b, d), lambda l, j: (0, 0)),
            pl.BlockSpec((1, d, d), lambda l, j: (l, 0, j)),
            pl.BlockSpec((1, 1, d), lambda l, j: (l, 0, j)),
            vspec, vspec, vspec, vspec,
            pl.BlockSpec((1, 1, LANES), lambda l, j: (l, 0, 0)),
        ],
        out_specs=[
            pl.BlockSpec((1, b, d), lambda l, j: (l, 0, j)),
            pl.BlockSpec((1, 1, LANES), lambda l, j: (l, 0, 0)),
        ],
        out_shape=[
            jax.ShapeDtypeStruct((depth, b, d6), F32),
            jax.ShapeDtypeStruct((depth, 1, LANES), F32),
        ],
        compiler_params=_params(("arbitrary", "arbitrary")),
        name="adaln_mod",
    )(c, w_ada, b_ada.reshape(depth, 1, d6), vec(lam_q1), vec(lam_k1), vec(lam_q2), vec(lam_k2), lam_init)


def _rope_kernel(pos_ref, invf_ref, c_ref, s_ref):
    ang = pos_ref[0].astype(F32) * invf_ref[...]
    d = lax.broadcasted_iota(jnp.int32, ang.shape, 1) % HEAD_DIM
    cos = jnp.cos(ang)
    sin = jnp.sin(ang)
    c_ref[0] = jnp.where(d < ROPE_DIM, cos, 1.0)
    s_ref[0] = jnp.where(d < ROPE_HALF, -sin, jnp.where(d < ROPE_DIM, sin, 0.0))


def _rope_call(positions):
    b, s = positions.shape
    ts = 512
    inv_freq = ROPE_THETA ** (-jnp.arange(0, ROPE_DIM, 2, dtype=F32) / ROPE_DIM)
    invf = jnp.tile(inv_freq, LANES // ROPE_HALF).reshape(1, LANES)
    out = jax.ShapeDtypeStruct((b, s, LANES), F32)
    return pl.pallas_call(
        _rope_kernel,
        grid=(b, s // ts),
        in_specs=[
            pl.BlockSpec((1, ts, 1), lambda i, j: (i, j, 0)),
            pl.BlockSpec((1, LANES), lambda i, j: (0, 0)),
        ],
        out_specs=[pl.BlockSpec((1, ts, LANES), lambda i, j: (i, j, 0))] * 2,
        out_shape=[out, out],
        compiler_params=_params(("parallel", "parallel")),
        name="rope_tables",
    )(positions.reshape(b, s, 1), invf)


def _modulated_norm(x, g, shift, scale):
    ms = jnp.mean(x * x, axis=-1, keepdims=True)
    return (x * lax.rsqrt(ms + EPS) * g) * (1.0 + scale) + shift


def _group_mean_sq(y, r_ref):
    return _dot((y * y).astype(BF16), r_ref[...])


def _in_proj_kernel(x_ref, mod_ref, g_ref, wqk_ref, wv_ref, r_ref, gqk_ref, rc_ref, rs_ref,
                    dq_ref, dk_ref, lq_ref, lk_ref, dv_ref, lv_ref):
    d = x_ref.shape[-1]
    mod = mod_ref[0]
    h = _modulated_norm(x_ref[0], g_ref[...], mod[:, 0:d], mod[:, d:2 * d]).astype(BF16)
    rope_c = rc_ref[0]
    rope_s = rs_ref[0]
    first_half = (lax.broadcasted_iota(jnp.int32, rope_c.shape, 1) % HEAD_DIM) < ROPE_HALF
    pqk = _dot(h, wqk_ref[...])
    outs = (dq_ref, dk_ref, lq_ref, lk_ref)
    per_out = DIFF_WIDTH // MXU_DIM
    for c in range(pqk.shape[1] // MXU_DIM):
        xc = pqk[:, c * MXU_DIM:(c + 1) * MXU_DIM]
        r = lax.rsqrt(_group_mean_sq(xc, r_ref) + EPS)
        y = xc * r * gqk_ref[:, c * MXU_DIM:(c + 1) * MXU_DIM]
        o_ref = outs[c // per_out]
        for t in range(MXU_DIM // LANES):
            yt = y[:, t * LANES:(t + 1) * LANES]
            partner = jnp.where(first_half, pltpu.roll(yt, LANES - ROPE_HALF, 1), pltpu.roll(yt, ROPE_HALF, 1))
            col = (c % per_out) * MXU_DIM + t * LANES
            o_ref[0, :, col:col + LANES] = (yt * rope_c + partner * rope_s).astype(BF16)
    pv = _dot(h, wv_ref[...]).astype(BF16)
    dv_ref[0] = pv[:, :DIFF_WIDTH]
    lv_ref[0] = pv[:, DIFF_WIDTH:]


def _in_proj_call(x, mod_l, g, wqk, wv, rmat, gqk, rope_c, rope_s):
    b, s, d = x.shape
    tm = TOKEN_TILE
    tok = lambda w: pl.BlockSpec((1, tm, w), lambda i, j: (i, j, 0))
    o512 = jax.ShapeDtypeStruct((b, s, DIFF_WIDTH), BF16)
    return pl.pallas_call(
        _in_proj_kernel,
        grid=(b, s // tm),
        in_specs=[
            tok(d),
            pl.BlockSpec((1, 1, mod_l.shape[-1]), lambda i, j: (i, 0, 0)),
            _const_spec((1, d)),
            _const_spec(wqk.shape, True), _const_spec(wv.shape, True), _const_spec(rmat.shape),
            _const_spec(gqk.shape),
            tok(LANES), tok(LANES),
        ],
        out_specs=[tok(DIFF_WIDTH)] * 6,
        out_shape=[o512] * 6,
        compiler_params=_params(("parallel", "parallel")),
        name="in_proj",
    )(x, mod_l, g, wqk, wv, rmat, gqk, rope_c, rope_s)


def _diff_attn_kernel(bound_ref, q_ref, k_ref, v_ref, lam_ref, g_ref, o_ref, acc_ref):
    tq = q_ref.shape[1]
    seq = k_ref.shape[1]
    qi = pl.program_id(2)
    q = q_ref[0]
    lane = lax.broadcasted_iota(jnp.int32, q.shape, 1)
    zero = jnp.zeros_like(q)
    qb = jnp.concatenate([jnp.where(lane < HEAD_DIM, q, zero), jnp.where(lane >= HEAD_DIM, q, zero)], axis=0)

    def causal(s, k0):
        qry = lax.broadcasted_iota(jnp.int32, s.shape, 0) % tq
        key = lax.broadcasted_iota(jnp.int32, s.shape, 1) + k0
        return jnp.where(key <= qry, s, NEG_INF)

    def v_ext(k0, n):
        return jnp.concatenate([v_ref[0, pl.ds(k0, n), :], jnp.ones((n, LANES), BF16)], axis=1)

    def unshifted():
        for blk in range(seq // tq):
            @pl.when(qi == blk)
            def _(blk=blk):
                kv = (blk + 1) * tq
                s = causal(_dot_nt(qb, k_ref[0, :kv, :]), -blk * tq)
                acc_ref[...] = _dot(jnp.exp2(s).astype(BF16), v_ext(0, kv))

    def shifted():
        def step(j, carry, diagonal):
            m, acc = carry
            k0 = pl.multiple_of(j * tq, tq)
            s = _dot_nt(qb, k_ref[0, pl.ds(k0, tq), :])
            if diagonal:
                s = causal(s, 0)
            m_new = jnp.maximum(m, jnp.max(s, axis=-1, keepdims=True))
            acc = jnp.exp2(m - m_new) * acc + _dot(jnp.exp2(s - m_new).astype(BF16), v_ext(k0, tq))
            return m_new, acc

        carry = (jnp.full((2 * tq, 1), NEG_INF, F32), jnp.zeros((2 * tq, 2 * LANES), F32))
        carry = lax.fori_loop(0, qi, lambda j, c: step(j, c, False), carry)
        acc_ref[...] = step(qi, carry, True)[1]

    safe = bound_ref[0] <= SAFE_LOGIT
    pl.when(safe)(unshifted)
    pl.when(jnp.logical_not(safe))(shifted)

    acc = acc_ref[...]
    o1 = acc[:tq, :LANES] / acc[:tq, LANES:]
    o2 = acc[tq:, :LANES] / acc[tq:, LANES:]
    o = o1 - lam_ref[0:1, 0:1] * o2
    ms = jnp.mean(o * o, axis=-1, keepdims=True)
    o_ref[0] = (o * lax.rsqrt(ms + EPS) * g_ref[...]).astype(o_ref.dtype)


def _diff_attn_call(bound, dq, dk, dv, lam_l, g_sub):
    b, s, w = dq.shape
    tq = TOKEN_TILE
    return pl.pallas_call(
        _diff_attn_kernel,
        grid=(b, w // LANES, s // tq),
        in_specs=[
            _SMEM_SPEC,
            pl.BlockSpec((1, tq, LANES), lambda i, h, j: (i, j, h)),
            pl.BlockSpec((1, s, LANES), lambda i, h, j: (i, 0, h)),
            pl.BlockSpec((1, s, LANES), lambda i, h, j: (i, 0, h)),
            _const_spec((1, LANES)),
            _const_spec((1, LANES)),
        ],
        out_specs=pl.BlockSpec((1, tq, LANES), lambda i, h, j: (i, j, h)),
        out_shape=jax.ShapeDtypeStruct((b, s, w), BF16),
        scratch_shapes=[pltpu.VMEM((2 * tq, 2 * LANES), F32)],
        compiler_params=_params(("parallel", "parallel", "arbitrary")),
        name="diff_attn",
    )(bound, dq, dk, dv, lam_l, g_sub)


def _rows(start, size, stride):
    return pl.ds(start, size, stride=stride) if stride > 1 else pl.ds(start, size)


def _dil_attn_kernel(bound_ref, q_ref, k_ref, v_ref, r_ref, g_ref, o_ref, qf, kf, vf, num, den, mx):
    seq = q_ref.shape[1]
    blk = DIL_BLOCK
    qf[...] = q_ref[0].astype(F32)
    kf[...] = k_ref[0].astype(F32)
    vf[...] = v_ref[0].astype(F32)
    head0 = lax.broadcasted_iota(jnp.int32, (blk, LANES), 1) < HEAD_DIM

    def window(ctx):
        qq = lax.broadcasted_iota(jnp.int32, (2 * blk, ctx), 0) % blk
        kk = lax.broadcasted_iota(jnp.int32, (2 * blk, ctx), 1)
        return (kk <= qq) if ctx == blk else ((kk >= qq) & (kk <= qq + blk))

    def block(stride, q0, k0, ctx, mode, first):
        qsel = _rows(q0, blk, stride)
        ksel = _rows(k0, ctx, stride)
        q = qf[qsel, :].astype(BF16)
        zero = jnp.zeros_like(q)
        qb = jnp.concatenate([jnp.where(head0, q, zero), jnp.where(head0, zero, q)], axis=0)
        s = _dot_nt(qb, kf[ksel, :].astype(BF16))
        s = jnp.where(window(ctx), s, NEG_INF)
        if mode == "max":
            m = jnp.max(s, axis=-1, keepdims=True)
            m = jnp.where(head0, m[:blk], m[blk:])
            mx[qsel, :] = m if first else jnp.maximum(mx[qsel, :], m)
            return
        if mode == "shifted":
            m = mx[qsel, :]
            s = s - jnp.concatenate([m[:, 0:1], m[:, HEAD_DIM:HEAD_DIM + 1]], axis=0)
        p = jnp.exp2(s).astype(BF16)
        vext = jnp.concatenate([vf[ksel, :].astype(BF16), jnp.ones((ctx, LANES), BF16)], axis=1)
        acc = _dot(p, vext)
        n_blk = jnp.where(head0, acc[:blk, :LANES], acc[blk:, :LANES])
        d_blk = jnp.where(head0, acc[:blk, LANES:], acc[blk:, LANES:])
        if first:
            num[qsel, :] = n_blk
            den[qsel, :] = d_blk
        else:
            num[qsel, :] = num[qsel, :] + n_blk
            den[qsel, :] = den[qsel, :] + d_blk

    def sweep(mode, blocks_per_body):
        for _, stride in DILATED_PAIRS:
            nb = seq // stride // blk
            first = stride == DILATED_PAIRS[0][1]

            def subsequence(r, stride=stride, nb=nb, first=first):
                block(stride, r, r, blk, mode, first)
                for n in range(1, nb):
                    q0 = r + n * (blk * stride)
                    block(stride, q0, q0 - blk * stride, 2 * blk, mode, first)

            per_body = max(1, min(stride, blocks_per_body // nb))
            if per_body == stride:
                for r in range(stride):
                    subsequence(r)
            else:
                def body(g, _, subsequence=subsequence, per_body=per_body):
                    for u in range(per_body):
                        subsequence(g * per_body + u)
                    return 0
                lax.fori_loop(0, stride // per_body, body, 0)

    safe = bound_ref[0] <= SAFE_LOGIT

    @pl.when(safe)
    def _():
        sweep("plain", seq // blk)

    @pl.when(jnp.logical_not(safe))
    def _():
        sweep("max", 1)
        sweep("shifted", 1)

    o = num[...] / den[...]
    ms = _dot((o * o).astype(BF16), r_ref[...])
    o_ref[0] = (o * lax.rsqrt(ms + EPS) * g_ref[...]).astype(o_ref.dtype)


def _dil_attn_call(bound, lq, lk, lv, rmat, gd):
    b, s, w = lq.shape
    spec = pl.BlockSpec((1, s, LANES), lambda i, h: (i, 0, h))
    scratch = pltpu.VMEM((s, LANES), F32)
    return pl.pallas_call(
        _dil_attn_kernel,
        grid=(b, w // LANES),
        in_specs=[_SMEM_SPEC, spec, spec, spec, _const_spec((LANES, LANES)),
                  pl.BlockSpec((1, LANES), lambda i, h: (0, h))],
        out_specs=spec,
        out_shape=jax.ShapeDtypeStruct((b, s, w), BF16),
        scratch_shapes=[scratch] * 6,
        compiler_params=_params(("parallel", "parallel")),
        name="dil_attn",
    )(bound, lq, lk, lv, rmat, gd)


def _out_proj_kernel(x_ref, mod_ref, oa_ref, ob_ref, wa_ref, wb_ref, xo_ref):
    d = x_ref.shape[-1]
    y = _dot(oa_ref[0], wa_ref[...]) + _dot(ob_ref[0], wb_ref[...])
    xo_ref[0] = x_ref[0] + mod_ref[0][:, 2 * d:3 * d] * y


def _out_proj_call(x, mod_l, oa, ob, wa, wb):
    b, s, d = x.shape
    tm = TOKEN_TILE
    tok = lambda w: pl.BlockSpec((1, tm, w), lambda i, j: (i, j, 0))
    return pl.pallas_call(
        _out_proj_kernel,
        grid=(b, s // tm),
        in_specs=[tok(d), pl.BlockSpec((1, 1, mod_l.shape[-1]), lambda i, j: (i, 0, 0)),
                  tok(DIFF_WIDTH), tok(DIL_WIDTH), _const_spec(wa.shape, True), _const_spec(wb.shape, True)],
        out_specs=tok(d),
        out_shape=jax.ShapeDtypeStruct(x.shape, x.dtype),
        compiler_params=_params(("parallel", "parallel")),
        name="out_proj",
    )(x, mod_l, oa, ob, wa, wb)


def _ffn_kernel(x_ref, mod_ref, g_ref, wg_ref, wv_ref, cw_ref, cb_ref, wd_ref, xo_ref, tail_ref):
    d = x_ref.shape[-1]
    tm = x_ref.shape[1]
    nchunk = wg_ref.shape[0]
    halo = tail_ref.shape[2]

    @pl.when(pl.program_id(1) == 0)
    def _():
        tail_ref[...] = jnp.zeros_like(tail_ref)

    x = x_ref[0]
    mod = mod_ref[0]
    h = _modulated_norm(x, g_ref[...], mod[:, 3 * d:4 * d], mod[:, 4 * d:5 * d]).astype(BF16)
    row = lax.broadcasted_iota(jnp.int32, (halo, FF_CHUNK), 0)

    def chunk(c):
        convs = []
        for part, w_ref in enumerate((wg_ref, wv_ref)):
            u = _dot(h, w_ref[c])
            prev = tail_ref[c, part]
            tail_ref[c, part] = u[tm - halo:, :]
            cw = cw_ref[c, part]
            conv = cb_ref[c, part] + u * cw[CONV_WIDTH - 1:CONV_WIDTH, :]
            for lag in range(1, CONV_WIDTH):
                shifted = pltpu.roll(u, lag, 0)
                head = jnp.where(row < lag, pltpu.roll(prev, lag, 0), shifted[:halo])
                shifted = jnp.concatenate([head, shifted[halo:]], axis=0)
                conv = conv + shifted * cw[CONV_WIDTH - 1 - lag:CONV_WIDTH - lag, :]
            convs.append(conv)
        half_gate, val = convs
        return ((half_gate + half_gate * jnp.tanh(half_gate)) * val).astype(BF16)

    y = None
    for c0 in range(0, nchunk, FF_DOWN_GROUP):
        cs = range(c0, min(c0 + FF_DOWN_GROUP, nchunk))
        act = jnp.concatenate([chunk(c) for c in cs], axis=1)
        part = _dot(act, jnp.concatenate([wd_ref[c] for c in cs], axis=0))
        y = part if y is None else y + part
    xo_ref[0] = x + mod[:, 5 * d:6 * d] * y


def _ffn_call(x, mod_l, g, wg, wv, cw, cb, wd):
    b, s, d = x.shape
    tm = TOKEN_TILE
    tok = pl.BlockSpec((1, tm, d), lambda i, j: (i, j, 0))
    return pl.pallas_call(
        _ffn_kernel,
        grid=(b, s // tm),
        in_specs=[tok, pl.BlockSpec((1, 1, mod_l.shape[-1]), lambda i, j: (i, 0, 0)), _const_spec((1, d)),
                  _const_spec(wg.shape, True), _const_spec(wv.shape, True), _const_spec(cw.shape),
                  _const_spec(cb.shape), _const_spec(wd.shape, True)],
        out_specs=tok,
        out_shape=jax.ShapeDtypeStruct(x.shape, x.dtype),
        scratch_shapes=[pltpu.VMEM((wg.shape[0], 2, SUBLANES, FF_CHUNK), F32)],
        compiler_params=_params(("parallel", "arbitrary")),
        name="conv_ffn",
    )(x, mod_l, g, wg, wv, cw, cb, wd)


def _group_mean_matrix(n):
    idx = np.arange(n) // HEAD_DIM
    return jnp.asarray((idx[:, None] == idx[None, :]).astype(np.float32) / HEAD_DIM, dtype=BF16)


def _logit_bound(g_q, g_k):
    return (HEAD_DIM * QK_SCALE * jnp.max(jnp.abs(g_q)) * jnp.max(jnp.abs(g_k))).reshape(1).astype(F32)


def kernel(x, c, positions, g_mix, g_ffn, w_ada, b_ada, w_in, w_out, diff_q_g, diff_k_g, lam_q1, lam_k1, lam_q2, lam_k2, diff_subln_g, dil_q_g, dil_k_g, dil_out_g, w_up, conv_w, conv_b, w_down):
    depth, d, _ = w_in.shape
    b = x.shape[0]
    d_ff = w_down.shape[1]
    nchunk = d_ff // FF_CHUNK

    lam_init = np.array([0.8 - 0.6 * math.exp(-0.3 * l) for l in range(depth)], np.float32)
    lam_init_tile = jnp.asarray(np.broadcast_to(lam_init[:, None, None], (depth, 1, LANES)))
    mod, lam = _mod_call(c, w_ada, b_ada, lam_q1, lam_k1, lam_q2, lam_k2, lam_init_tile)
    rope_c, rope_s = _rope_call(positions)
    rmat_qk = _group_mean_matrix(MXU_DIM)
    rmat_out = _group_mean_matrix(LANES)
    gate_half = jnp.asarray([0.5, 1.0], F32).reshape(1, 2, 1, 1)

    for l in range(depth):
        dq_w, dk_w, dv_w, lq_w, lk_w, lv_w = jnp.split(
            w_in[l], np.cumsum([DIFF_WIDTH, DIFF_WIDTH, DIFF_WIDTH, DIL_WIDTH, DIL_WIDTH]).tolist(), axis=1)
        wqk = jnp.concatenate([dq_w, dk_w, lq_w, lk_w], axis=1).astype(BF16)
        wv = jnp.concatenate([dv_w, lv_w], axis=1).astype(BF16)
        reps = DIFF_WIDTH // HEAD_DIM
        gqk = jnp.concatenate([jnp.tile(diff_q_g[l], reps) * QK_SCALE, jnp.tile(diff_k_g[l], reps),
                               jnp.tile(dil_q_g[l], reps) * QK_SCALE, jnp.tile(dil_k_g[l], reps)]).reshape(1, -1)
        mod_l = mod[l].reshape(b, 1, -1)

        dq, dk, lq, lk, dv, lv = _in_proj_call(x, mod_l, g_mix[l].reshape(1, d), wqk, wv, rmat_qk, gqk,
                                               rope_c, rope_s)
        g_sub = (diff_subln_g[l] * (1.0 - float(lam_init[l]))).reshape(1, DIFF_V_DIM)
        oa = _diff_attn_call(_logit_bound(diff_q_g[l], diff_k_g[l]), dq, dk, dv, lam[l], g_sub)
        gd = jnp.tile(dil_out_g[l], N_DIL_HEADS).reshape(1, DIL_WIDTH)
        ob = _dil_attn_call(_logit_bound(dil_q_g[l], dil_k_g[l]), lq, lk, lv, rmat_out, gd)
        wo = w_out[l].astype(BF16)
        x = _out_proj_call(x, mod_l, oa, ob, wo[:DIFF_WIDTH], wo[DIFF_WIDTH:])

        def chunked_cols(a):
            return a.reshape(a.shape[0], 2, nchunk, FF_CHUNK).transpose(2, 1, 0, 3)
        wup = chunked_cols(w_up[l].astype(BF16))
        cw = chunked_cols(conv_w[l]) * gate_half
        cb = chunked_cols(conv_b[l].reshape(1, -1)) * gate_half
        wd = w_down[l].astype(BF16).reshape(nchunk, FF_CHUNK, d)
        x = _ffn_call(x, mod_l, g_ffn[l].reshape(1, d), wup[:, 0], wup[:, 1], cw, cb, wd)
    return x
```

```python
import math

import jax
import jax.numpy as jnp
import numpy as np
from jax import lax
from jax.experimental import pallas as pl
from jax.experimental.pallas import tpu as pltpu

HEAD_DIM = 64
N_DIFF_HEADS = 4
DIFF_V_DIM = 2 * HEAD_DIM
DIFF_WIDTH = N_DIFF_HEADS * DIFF_V_DIM
N_DIL_HEADS = 8
DIL_WIDTH = N_DIL_HEADS * HEAD_DIM
DILATED_PAIRS = ((128, 1), (512, 4), (2048, 16))
ROPE_THETA = 500000.0
ROPE_DIM = HEAD_DIM // 4
ROPE_HALF = ROPE_DIM // 2
CONV_WIDTH = 3
EPS = 1e-6
NEG_INF = -1e30
LOG2E = 1.4426950408889634
QK_SCALE = HEAD_DIM ** -0.5 * LOG2E

LANES = 128
SUBLANES = 8
MXU_DIM = 256
TOKEN_TILE = 256
IN_PROJ_TILE = 512
DIFF_HEADS_PER_STEP = 2
DIL_BLOCK = 128
FF_CHUNK = 256
FF_DOWN_GROUP = 6
SAFE_LOGIT = 64.0
VMEM_LIMIT = 56 * 1024 * 1024

F32 = jnp.float32
BF16 = jnp.bfloat16


def _dot(a, b):
    return jnp.dot(a, b, preferred_element_type=F32)


def _dot_nt(a, b):
    return lax.dot_general(a, b, (((1,), (1,)), ((), ())), preferred_element_type=F32)


def _split_bf16(x):
    hi = x.astype(BF16)
    lo = (x - hi.astype(F32)).astype(BF16)
    return hi, lo


def _params(sem, vmem=VMEM_LIMIT):
    return pltpu.CompilerParams(dimension_semantics=sem, vmem_limit_bytes=vmem)


def _const_spec(shape, single_buffer=False):
    mode = pl.Buffered(1) if single_buffer else None
    return pl.BlockSpec(shape, lambda *_: (0,) * len(shape), pipeline_mode=mode)


_SMEM_SPEC = pl.BlockSpec(memory_space=pltpu.SMEM)


def _mod_kernel(c_ref, w_ref, b_ref, q1_ref, k1_ref, q2_ref, k2_ref, li_ref, mod_ref, lam_ref):
    c = c_ref[...]
    cond = c * (1.0 / (1.0 + jnp.exp(-c)))
    ch, cl = _split_bf16(cond)
    wh, wl = _split_bf16(w_ref[0])
    mod_ref[0] = _dot(ch, wh) + (_dot(ch, wl) + _dot(cl, wh)) + b_ref[0]
    s1 = jnp.sum(q1_ref[0] * k1_ref[0], axis=-1, keepdims=True)
    s2 = jnp.sum(q2_ref[0] * k2_ref[0], axis=-1, keepdims=True)
    lam_ref[0] = (jnp.exp(s1) - jnp.exp(s2)) + li_ref[0]


def _mod_call(c, w_ada, b_ada, lam_q1, lam_k1, lam_q2, lam_k2, lam_init):
    depth, d, d6 = w_ada.shape
    b = c.shape[0]
    nj = d6 // d
    vec = lambda a: a.reshape(depth, 1, HEAD_DIM)
    vspec = pl.BlockSpec((1, 1, HEAD_DIM), lambda l, j: (l, 0, 0))
    return pl.pallas_call(
        _mod_kernel,
        grid=(depth, nj),
        in_specs=[
            pl.BlockSpec((b, d), lambda l, j: (0, 0)),
            pl.BlockSpec((1, d, d), lambda l, j: (l, 0, j)),
            pl.BlockSpec((1, 1, d), lambda l, j: (l, 0, j)),
            vspec, vspec, vspec, vspec,
            pl.BlockSpec((1, 1, LANES), lambda l, j: (l, 0, 0)),
        ],
        out_specs=[
            pl.BlockSpec((1, b, d), lambda l, j: (l, 0, j)),
            pl.BlockSpec((1, 1, LANES), lambda l, j: (l, 0, 0)),
        ],
        out_shape=[
            jax.ShapeDtypeStruct((depth, b, d6), F32),
            jax.ShapeDtypeStruct((depth, 1, LANES), F32),
        ],
        compiler_params=_params(("arbitrary", "arbitrary")),
        name="adaln_mod",
    )(c, w_ada, b_ada.reshape(depth, 1, d6), vec(lam_q1), vec(lam_k1), vec(lam_q2), vec(lam_k2), lam_init)


def _rope_kernel(pos_ref, invf_ref, c_ref, s_ref):
    ang = pos_ref[0].astype(F32) * invf_ref[...]
    c_ref[0] = jnp.cos(ang)
    s_ref[0] = jnp.sin(ang)


def _rope_call(positions):
    b, s = positions.shape
    per_row = LANES // ROPE_HALF
    rows = s // per_row
    inv_freq = ROPE_THETA ** (-jnp.arange(0, ROPE_DIM, 2, dtype=F32) / ROPE_DIM)
    invf = jnp.tile(inv_freq, per_row).reshape(1, LANES)
    pos = jnp.repeat(positions.reshape(b, rows, per_row), ROPE_HALF, axis=-1)
    spec = pl.BlockSpec((1, rows, LANES), lambda i: (i, 0, 0))
    out = jax.ShapeDtypeStruct((b, rows, LANES), F32)
    cos, sin = pl.pallas_call(
        _rope_kernel,
        grid=(b,),
        in_specs=[spec, _const_spec((1, LANES))],
        out_specs=[spec, spec],
        out_shape=[out, out],
        compiler_params=_params(("parallel",)),
        name="rope_tables",
    )(pos, invf)
    lane_tile = lambda t: jnp.tile(t.reshape(b, s, ROPE_HALF), (1, 1, per_row))
    return lane_tile(cos), lane_tile(sin)


def _modulated_norm(x, g, shift, scale):
    ms = jnp.mean(x * x, axis=-1, keepdims=True)
    return (x * lax.rsqrt(ms + EPS) * g) * (1.0 + scale) + shift


def _group_mean_sq(y, r_ref):
    return _dot((y * y).astype(BF16), r_ref[...])


def _in_proj_kernel(x_ref, mod_ref, g_ref, w_ref, r_ref, gqk_ref, cos_ref, sin_ref,
                    dq_ref, dk_ref, dv_ref, lq_ref, lk_ref, lv_ref):
    d = x_ref.shape[-1]
    mod = mod_ref[0]
    h = _modulated_norm(x_ref[0], g_ref[...], mod[:, 0:d], mod[:, d:2 * d]).astype(BF16)
    dim = lax.broadcasted_iota(jnp.int32, cos_ref.shape[1:], 1) % HEAD_DIM
    first_half = dim < ROPE_HALF
    rope_c = jnp.where(dim < ROPE_DIM, cos_ref[0], 1.0)
    rope_s = jnp.where(first_half, -sin_ref[0], jnp.where(dim < ROPE_DIM, sin_ref[0], 0.0))
    width = dq_ref.shape[-1]
    for sec, o_ref in enumerate((dq_ref, dk_ref, dv_ref, lq_ref, lk_ref, lv_ref)):
        p = _dot(h, w_ref[:, sec * width:(sec + 1) * width])
        if o_ref is dv_ref or o_ref is lv_ref:
            o_ref[0] = p.astype(BF16)
            continue
        for c in range(width // MXU_DIM):
            col = sec * width + c * MXU_DIM
            xc = p[:, c * MXU_DIM:(c + 1) * MXU_DIM]
            y = xc * lax.rsqrt(_group_mean_sq(xc, r_ref) + EPS) * gqk_ref[:, col:col + MXU_DIM]
            for t in range(MXU_DIM // LANES):
                yt = y[:, t * LANES:(t + 1) * LANES]
                partner = jnp.where(first_half, pltpu.roll(yt, LANES - ROPE_HALF, 1), pltpu.roll(yt, ROPE_HALF, 1))
                out_col = c * MXU_DIM + t * LANES
                o_ref[0, :, out_col:out_col + LANES] = (yt * rope_c + partner * rope_s).astype(BF16)


def _in_proj_call(x, mod_l, g, w, rmat, gqk, cos_t, sin_t):
    b, s, d = x.shape
    tm = IN_PROJ_TILE
    tok = lambda w: pl.BlockSpec((1, tm, w), lambda i, j: (i, j, 0))
    width = w.shape[1] // 6
    out = jax.ShapeDtypeStruct((b, s, width), BF16)
    return pl.pallas_call(
        _in_proj_kernel,
        grid=(b, s // tm),
        in_specs=[
            tok(d),
            pl.BlockSpec((1, 1, mod_l.shape[-1]), lambda i, j: (i, 0, 0)),
            _const_spec((1, d)),
            _const_spec(w.shape, True), _const_spec(rmat.shape), _const_spec(gqk.shape),
            tok(LANES), tok(LANES),
        ],
        out_specs=[tok(width)] * 6,
        out_shape=[out] * 6,
        compiler_params=_params(("parallel", "parallel")),
        name="in_proj",
    )(x, mod_l, g, w, rmat, gqk, cos_t, sin_t)


def _diff_attn_kernel(bound_ref, q_ref, k_ref, v_ref, lam_ref, g_ref, o_ref, acc_ref):
    tq = q_ref.shape[1]
    seq = k_ref.shape[1]
    heads = q_ref.shape[2] // LANES
    qi = pl.program_id(2)
    lanes = lambda hh: slice(hh * LANES, (hh + 1) * LANES)

    def stacked_q(hh):
        q = q_ref[0, :, lanes(hh)]
        lane = lax.broadcasted_iota(jnp.int32, q.shape, 1)
        zero = jnp.zeros_like(q)
        return jnp.concatenate([jnp.where(lane < HEAD_DIM, q, zero), jnp.where(lane >= HEAD_DIM, q, zero)], axis=0)

    def causal(s, k0):
        qry = lax.broadcasted_iota(jnp.int32, s.shape, 0) % tq
        key = lax.broadcasted_iota(jnp.int32, s.shape, 1) + k0
        return jnp.where(key <= qry, s, NEG_INF)

    def v_ext(hh, k0, n):
        return jnp.concatenate([v_ref[0, pl.ds(k0, n), lanes(hh)], jnp.ones((n, LANES), BF16)], axis=1)

    def unshifted():
        for blk in range(seq // tq):
            @pl.when(qi == blk)
            def _(blk=blk):
                kv = (blk + 1) * tq
                for hh in range(heads):
                    s = causal(_dot_nt(stacked_q(hh), k_ref[0, :kv, lanes(hh)]), -blk * tq)
                    acc_ref[hh] = _dot(jnp.exp2(s).astype(BF16), v_ext(hh, 0, kv))

    def shifted():
        for hh in range(heads):
            qb = stacked_q(hh)

            def step(j, carry, diagonal, hh=hh, qb=qb):
                m, acc = carry
                k0 = pl.multiple_of(j * tq, tq)
                s = _dot_nt(qb, k_ref[0, pl.ds(k0, tq), lanes(hh)])
                if diagonal:
                    s = causal(s, 0)
                m_new = jnp.maximum(m, jnp.max(s, axis=-1, keepdims=True))
                acc = jnp.exp2(m - m_new) * acc + _dot(jnp.exp2(s - m_new).astype(BF16), v_ext(hh, k0, tq))
                return m_new, acc

            carry = (jnp.full((2 * tq, 1), NEG_INF, F32), jnp.zeros((2 * tq, 2 * LANES), F32))
            carry = lax.fori_loop(0, qi, lambda j, c, step=step: step(j, c, False), carry)
            acc_ref[hh] = step(qi, carry, True)[1]

    safe = bound_ref[0] <= SAFE_LOGIT
    pl.when(safe)(unshifted)
    pl.when(jnp.logical_not(safe))(shifted)

    for hh in range(heads):
        acc = acc_ref[hh]
        o1 = acc[:tq, :LANES] / acc[:tq, LANES:]
        o2 = acc[tq:, :LANES] / acc[tq:, LANES:]
        o = o1 - lam_ref[0:1, 0:1] * o2
        ms = jnp.mean(o * o, axis=-1, keepdims=True)
        o_ref[0, :, lanes(hh)] = (o * lax.rsqrt(ms + EPS) * g_ref[...]).astype(o_ref.dtype)


def _diff_attn_call(bound, dq, dk, dv, lam_l, g_sub):
    b, s, w = dq.shape
    tq = TOKEN_TILE
    cols = DIFF_HEADS_PER_STEP * LANES
    return pl.pallas_call(
        _diff_attn_kernel,
        grid=(b, w // cols, s // tq),
        in_specs=[
            _SMEM_SPEC,
            pl.BlockSpec((1, tq, cols), lambda i, h, j: (i, j, h)),
            pl.BlockSpec((1, s, cols), lambda i, h, j: (i, 0, h)),
            pl.BlockSpec((1, s, cols), lambda i, h, j: (i, 0, h)),
            _const_spec((1, LANES)),
            _const_spec((1, LANES)),
        ],
        out_specs=pl.BlockSpec((1, tq, cols), lambda i, h, j: (i, j, h)),
        out_shape=jax.ShapeDtypeStruct((b, s, w), BF16),
        scratch_shapes=[pltpu.VMEM((DIFF_HEADS_PER_STEP, 2 * tq, 2 * LANES), F32)],
        compiler_params=_params(("parallel", "parallel", "arbitrary")),
        name="diff_attn",
    )(bound, dq, dk, dv, lam_l, g_sub)


def _rows(start, size, stride):
    return pl.ds(start, size, stride=stride) if stride > 1 else pl.ds(start, size)


def _dil_attn_kernel(bound_ref, q_ref, k_ref, v_ref, r_ref, g_ref, o_ref, qf, kf, vf, num, den, mx):
    seq = q_ref.shape[1]
    blk = DIL_BLOCK
    qf[...] = q_ref[0].astype(F32)
    kf[...] = k_ref[0].astype(F32)
    vf[...] = v_ref[0].astype(F32)
    head0 = lax.broadcasted_iota(jnp.int32, (blk, LANES), 1) < HEAD_DIM

    def window(ctx):
        qq = lax.broadcasted_iota(jnp.int32, (2 * blk, ctx), 0) % blk
        kk = lax.broadcasted_iota(jnp.int32, (2 * blk, ctx), 1)
        return (kk <= qq) if ctx == blk else ((kk >= qq) & (kk <= qq + blk))

    def block(stride, q0, k0, ctx, mode, first):
        qsel = _rows(q0, blk, stride)
        ksel = _rows(k0, ctx, stride)
        q = qf[qsel, :].astype(BF16)
        zero = jnp.zeros_like(q)
        qb = jnp.concatenate([jnp.where(head0, q, zero), jnp.where(head0, zero, q)], axis=0)
        s = _dot_nt(qb, kf[ksel, :].astype(BF16))
        s = jnp.where(window(ctx), s, NEG_INF)
        if mode == "max":
            m = jnp.max(s, axis=-1, keepdims=True)
            m = jnp.where(head0, m[:blk], m[blk:])
            mx[qsel, :] = m if first else jnp.maximum(mx[qsel, :], m)
            return
        if mode == "shifted":
            m = mx[qsel, :]
            s = s - jnp.concatenate([m[:, 0:1], m[:, HEAD_DIM:HEAD_DIM + 1]], axis=0)
        p = jnp.exp2(s).astype(BF16)
        vext = jnp.concatenate([vf[ksel, :].astype(BF16), jnp.ones((ctx, LANES), BF16)], axis=1)
        acc = _dot(p, vext)
        n_blk = jnp.where(head0, acc[:blk, :LANES], acc[blk:, :LANES])
        d_blk = jnp.where(head0, acc[:blk, LANES:], acc[blk:, LANES:])
        if first:
            num[qsel, :] = n_blk
            den[qsel, :] = d_blk
        else:
            num[qsel, :] = num[qsel, :] + n_blk
            den[qsel, :] = den[qsel, :] + d_blk

    def sweep(mode, blocks_per_body):
        for _, stride in DILATED_PAIRS:
            nb = seq // stride // blk
            first = stride == DILATED_PAIRS[0][1]

            def subsequence(r, stride=stride, nb=nb, first=first):
                block(stride, r, r, blk, mode, first)
                for n in range(1, nb):
                    q0 = r + n * (blk * stride)
                    block(stride, q0, q0 - blk * stride, 2 * blk, mode, first)

            per_body = max(1, min(stride, blocks_per_body // nb))
            if per_body == stride:
                for r in range(stride):
                    subsequence(r)
            else:
                def body(g, _, subsequence=subsequence, per_body=per_body):
                    for u in range(per_body):
                        subsequence(g * per_body + u)
                    return 0
                lax.fori_loop(0, stride // per_body, body, 0)

    safe = bound_ref[0] <= SAFE_LOGIT

    @pl.when(safe)
    def _():
        sweep("plain", seq // blk)

    @pl.when(jnp.logical_not(safe))
    def _():
        sweep("max", 1)
        sweep("shifted", 1)

    o = num[...] / den[...]
    ms = _dot((o * o).astype(BF16), r_ref[...])
    o_ref[0] = (o * lax.rsqrt(ms + EPS) * g_ref[...]).astype(o_ref.dtype)


def _dil_attn_call(bound, lq, lk, lv, rmat, gd):
    b, s, w = lq.shape
    spec = pl.BlockSpec((1, s, LANES), lambda i, h: (i, 0, h))
    scratch = pltpu.VMEM((s, LANES), F32)
    return pl.pallas_call(
        _dil_attn_kernel,
        grid=(b, w // LANES),
        in_specs=[_SMEM_SPEC, spec, spec, spec, _const_spec((LANES, LANES)),
                  pl.BlockSpec((1, LANES), lambda i, h: (0, h))],
        out_specs=spec,
        out_shape=jax.ShapeDtypeStruct((b, s, w), BF16),
        scratch_shapes=[scratch] * 6,
        compiler_params=_params(("parallel", "parallel")),
        name="dil_attn",
    )(bound, lq, lk, lv, rmat, gd)


def _mix_ffn_kernel(x_ref, mod_ref, oa_ref, ob_ref, wo_ref, g_ref, wu_ref, cw_ref, cb_ref, wd_ref,
                    xo_ref, tail_ref):
    d = x_ref.shape[-1]
    tm = x_ref.shape[1]
    d_ff = wd_ref.shape[0]
    nchunk = d_ff // FF_CHUNK
    halo = tail_ref.shape[2]
    wa = oa_ref.shape[-1]

    @pl.when(pl.program_id(1) == 0)
    def _():
        tail_ref[...] = jnp.zeros_like(tail_ref)

    mod = mod_ref[0]
    mixed = _dot(oa_ref[0], wo_ref[:wa, :]) + _dot(ob_ref[0], wo_ref[wa:, :])
    x = x_ref[0] + mod[:, 2 * d:3 * d] * mixed
    h = _modulated_norm(x, g_ref[...], mod[:, 3 * d:4 * d], mod[:, 4 * d:5 * d]).astype(BF16)
    row = lax.broadcasted_iota(jnp.int32, (halo, FF_CHUNK), 0)

    def chunk(c):
        convs = []
        for part in range(2):
            cols = slice(part * d_ff + c * FF_CHUNK, part * d_ff + (c + 1) * FF_CHUNK)
            u = _dot(h, wu_ref[:, cols])
            prev = tail_ref[c, part]
            tail_ref[c, part] = u[tm - halo:, :]
            conv = cb_ref[:, cols] + u * cw_ref[CONV_WIDTH - 1:CONV_WIDTH, cols]
            for lag in range(1, CONV_WIDTH):
                shifted = pltpu.roll(u, lag, 0)
                head = jnp.where(row < lag, pltpu.roll(prev, lag, 0), shifted[:halo])
                shifted = jnp.concatenate([head, shifted[halo:]], axis=0)
                conv = conv + shifted * cw_ref[CONV_WIDTH - 1 - lag:CONV_WIDTH - lag, cols]
            convs.append(conv)
        half_gate, val = convs
        return ((half_gate + half_gate * jnp.tanh(half_gate)) * val).astype(BF16)

    y = None
    for c0 in range(0, nchunk, FF_DOWN_GROUP):
        c1 = min(c0 + FF_DOWN_GROUP, nchunk)
        act = jnp.concatenate([chunk(c) for c in range(c0, c1)], axis=1)
        part = _dot(act, wd_ref[c0 * FF_CHUNK:c1 * FF_CHUNK, :])
        y = part if y is None else y + part
    xo_ref[0] = x + mod[:, 5 * d:6 * d] * y


def _mix_ffn_call(x, mod_l, oa, ob, wo, g, wu, cw, cb, wd):
    b, s, d = x.shape
    tm = TOKEN_TILE
    tok = lambda w: pl.BlockSpec((1, tm, w), lambda i, j: (i, j, 0))
    nchunk = wd.shape[0] // FF_CHUNK
    return pl.pallas_call(
        _mix_ffn_kernel,
        grid=(b, s // tm),
        in_specs=[tok(d), pl.BlockSpec((1, 1, mod_l.shape[-1]), lambda i, j: (i, 0, 0)),
                  tok(oa.shape[-1]), tok(ob.shape[-1]), _const_spec(wo.shape, True), _const_spec((1, d)),
                  _const_spec(wu.shape, True), _const_spec(cw.shape), _const_spec(cb.shape),
                  _const_spec(wd.shape, True)],
        out_specs=tok(d),
        out_shape=jax.ShapeDtypeStruct(x.shape, x.dtype),
        scratch_shapes=[pltpu.VMEM((nchunk, 2, SUBLANES, FF_CHUNK), F32)],
        compiler_params=_params(("parallel", "arbitrary")),
        name="mix_ffn",
    )(x, mod_l, oa, ob, wo, g, wu, cw, cb, wd)


def _group_mean_matrix(n):
    idx = np.arange(n) // HEAD_DIM
    return jnp.asarray((idx[:, None] == idx[None, :]).astype(np.float32) / HEAD_DIM, dtype=BF16)


def _logit_bound(g_q, g_k):
    return (HEAD_DIM * QK_SCALE * jnp.max(jnp.abs(g_q)) * jnp.max(jnp.abs(g_k))).reshape(1).astype(F32)


def kernel(x, c, positions, g_mix, g_ffn, w_ada, b_ada, w_in, w_out, diff_q_g, diff_k_g, lam_q1, lam_k1, lam_q2, lam_k2, diff_subln_g, dil_q_g, dil_k_g, dil_out_g, w_up, conv_w, conv_b, w_down):
    depth, d, _ = w_in.shape
    b = x.shape[0]
    d_ff = w_down.shape[1]

    lam_init = np.array([0.8 - 0.6 * math.exp(-0.3 * l) for l in range(depth)], np.float32)
    lam_init_tile = jnp.asarray(np.broadcast_to(lam_init[:, None, None], (depth, 1, LANES)))
    mod, lam = _mod_call(c, w_ada, b_ada, lam_q1, lam_k1, lam_q2, lam_k2, lam_init_tile)
    cos_t, sin_t = _rope_call(positions)
    rmat_qk = _group_mean_matrix(MXU_DIM)
    rmat_out = _group_mean_matrix(LANES)
    gate_half = jnp.concatenate([jnp.full((1, d_ff), 0.5, F32), jnp.ones((1, d_ff), F32)], axis=1)
    reps = DIFF_WIDTH // HEAD_DIM
    ones = jnp.ones((DIFF_WIDTH,), F32)

    for l in range(depth):
        gqk = jnp.concatenate([jnp.tile(diff_q_g[l], reps) * QK_SCALE, jnp.tile(diff_k_g[l], reps), ones,
                               jnp.tile(dil_q_g[l], reps) * QK_SCALE, jnp.tile(dil_k_g[l], reps), ones]).reshape(1, -1)
        mod_l = mod[l].reshape(b, 1, -1)
        dq, dk, dv, lq, lk, lv = _in_proj_call(x, mod_l, g_mix[l].reshape(1, d), w_in[l].astype(BF16), rmat_qk, gqk,
                                               cos_t, sin_t)
        g_sub = (diff_subln_g[l] * (1.0 - float(lam_init[l]))).reshape(1, DIFF_V_DIM)
        oa = _diff_attn_call(_logit_bound(diff_q_g[l], diff_k_g[l]), dq, dk, dv, lam[l], g_sub)
        gd = jnp.tile(dil_out_g[l], N_DIL_HEADS).reshape(1, DIL_WIDTH)
        ob = _dil_attn_call(_logit_bound(dil_q_g[l], dil_k_g[l]), lq, lk, lv, rmat_out, gd)
        x = _mix_ffn_call(x, mod_l, oa, ob, w_out[l].astype(BF16), g_ffn[l].reshape(1, d), w_up[l].astype(BF16),
                          conv_w[l] * gate_half, conv_b[l].reshape(1, -1) * gate_half, w_down[l].astype(BF16))
    return x
```

```python
import math

import jax
import jax.numpy as jnp
import numpy as np
from jax import lax
from jax.experimental import pallas as pl
from jax.experimental.pallas import tpu as pltpu

HEAD_DIM = 64
N_DIFF_HEADS = 4
DIFF_V_DIM = 2 * HEAD_DIM
DIFF_WIDTH = N_DIFF_HEADS * DIFF_V_DIM
N_DIL_HEADS = 8
DIL_WIDTH = N_DIL_HEADS * HEAD_DIM
DILATED_PAIRS = ((128, 1), (512, 4), (2048, 16))
ROPE_THETA = 500000.0
ROPE_DIM = HEAD_DIM // 4
ROPE_HALF = ROPE_DIM // 2
CONV_WIDTH = 3
EPS = 1e-6
NEG_INF = -1e30
LOG2E = 1.4426950408889634
QK_SCALE = HEAD_DIM ** -0.5 * LOG2E

LANES = 128
SUBLANES = 8
MXU_DIM = 256
TOKEN_TILE = 256
IN_PROJ_TILE = 512
DIFF_HEADS_PER_STEP = 4
DIL_BLOCK = 128
FF_CHUNK = 256
FF_DOWN_GROUP = 6
SAFE_LOGIT = 64.0
VMEM_LIMIT = 56 * 1024 * 1024

F32 = jnp.float32
BF16 = jnp.bfloat16


def _dot(a, b):
    return jnp.dot(a, b, preferred_element_type=F32)


def _dot_nt(a, b):
    return lax.dot_general(a, b, (((1,), (1,)), ((), ())), preferred_element_type=F32)


def _split_bf16(x):
    hi = x.astype(BF16)
    lo = (x - hi.astype(F32)).astype(BF16)
    return hi, lo


def _params(sem, vmem=VMEM_LIMIT):
    return pltpu.CompilerParams(dimension_semantics=sem, vmem_limit_bytes=vmem)


def _const_spec(shape, single_buffer=False):
    mode = pl.Buffered(1) if single_buffer else None
    return pl.BlockSpec(shape, lambda *_: (0,) * len(shape), pipeline_mode=mode)


_SMEM_SPEC = pl.BlockSpec(memory_space=pltpu.SMEM)


def _mod_kernel(c_ref, w_ref, b_ref, q1_ref, k1_ref, q2_ref, k2_ref, li_ref, mod_ref, lam_ref):
    c = c_ref[...]
    cond = c * (1.0 / (1.0 + jnp.exp(-c)))
    ch, cl = _split_bf16(cond)
    wh, wl = _split_bf16(w_ref[0])
    mod_ref[0] = _dot(ch, wh) + (_dot(ch, wl) + _dot(cl, wh)) + b_ref[0]
    s1 = jnp.sum(q1_ref[0] * k1_ref[0], axis=-1, keepdims=True)
    s2 = jnp.sum(q2_ref[0] * k2_ref[0], axis=-1, keepdims=True)
    lam_ref[0] = (jnp.exp(s1) - jnp.exp(s2)) + li_ref[0]


def _mod_call(c, w_ada, b_ada, lam_q1, lam_k1, lam_q2, lam_k2, lam_init):
    depth, d, d6 = w_ada.shape
    b = c.shape[0]
    nj = d6 // d
    vec = lambda a: a.reshape(depth, 1, HEAD_DIM)
    vspec = pl.BlockSpec((1, 1, HEAD_DIM), lambda l, j: (l, 0, 0))
    return pl.pallas_call(
        _mod_kernel,
        grid=(depth, nj),
        in_specs=[
            pl.BlockSpec((b, d), lambda l, j: (0, 0)),
            pl.BlockSpec((1, d, d), lambda l, j: (l, 0, j)),
            pl.BlockSpec((1, 1, d), lambda l, j: (l, 0, j)),
            vspec, vspec, vspec, vspec,
            pl.BlockSpec((1, 1, LANES), lambda l, j: (l, 0, 0)),
        ],
        out_specs=[
            pl.BlockSpec((1, b, d), lambda l, j: (l, 0, j)),
            pl.BlockSpec((1, 1, LANES), lambda l, j: (l, 0, 0)),
        ],
        out_shape=[
            jax.ShapeDtypeStruct((depth, b, d6), F32),
            jax.ShapeDtypeStruct((depth, 1, LANES), F32),
        ],
        compiler_params=_params(("arbitrary", "arbitrary")),
        name="adaln_mod",
    )(c, w_ada, b_ada.reshape(depth, 1, d6), vec(lam_q1), vec(lam_k1), vec(lam_q2), vec(lam_k2), lam_init)


def _rope_kernel(pos_ref, invf_ref, c_ref, s_ref):
    ang = pos_ref[0].astype(F32) * invf_ref[...]
    c_ref[0] = jnp.cos(ang)
    s_ref[0] = jnp.sin(ang)


def _rope_call(positions):
    b, s = positions.shape
    per_row = LANES // ROPE_HALF
    rows = s // per_row
    inv_freq = ROPE_THETA ** (-jnp.arange(0, ROPE_DIM, 2, dtype=F32) / ROPE_DIM)
    invf = jnp.tile(inv_freq, per_row).reshape(1, LANES)
    pos = jnp.repeat(positions.reshape(b, rows, per_row), ROPE_HALF, axis=-1)
    spec = pl.BlockSpec((1, rows, LANES), lambda i: (i, 0, 0))
    out = jax.ShapeDtypeStruct((b, rows, LANES), F32)
    cos, sin = pl.pallas_call(
        _rope_kernel,
        grid=(b,),
        in_specs=[spec, _const_spec((1, LANES))],
        out_specs=[spec, spec],
        out_shape=[out, out],
        compiler_params=_params(("parallel",)),
        name="rope_tables",
    )(pos, invf)
    lane_tile = lambda t: jnp.tile(t.reshape(b, s, ROPE_HALF), (1, 1, per_row))
    return lane_tile(cos), lane_tile(sin)


def _modulated_norm(x, g, shift, scale):
    ms = jnp.mean(x * x, axis=-1, keepdims=True)
    return (x * lax.rsqrt(ms + EPS) * g) * (1.0 + scale) + shift


def _group_mean_sq(y, r_ref):
    return _dot((y * y).astype(BF16), r_ref[...])


def _in_proj_kernel(x_ref, mod_ref, g_ref, w_ref, r_ref, gqk_ref, cos_ref, sin_ref,
                    dq_ref, dk_ref, dv_ref, lq_ref, lk_ref, lv_ref):
    d = x_ref.shape[-1]
    mod = mod_ref[0]
    h = _modulated_norm(x_ref[0], g_ref[...], mod[:, 0:d], mod[:, d:2 * d]).astype(BF16)
    dim = lax.broadcasted_iota(jnp.int32, cos_ref.shape[1:], 1) % HEAD_DIM
    first_half = dim < ROPE_HALF
    rope_c = jnp.where(dim < ROPE_DIM, cos_ref[0], 1.0)
    rope_s = jnp.where(first_half, -sin_ref[0], jnp.where(dim < ROPE_DIM, sin_ref[0], 0.0))
    width = dq_ref.shape[-1]
    for sec, o_ref in enumerate((dq_ref, dk_ref, dv_ref, lq_ref, lk_ref, lv_ref)):
        p = _dot(h, w_ref[:, sec * width:(sec + 1) * width])
        if o_ref is dv_ref or o_ref is lv_ref:
            o_ref[0] = p.astype(BF16)
            continue
        for c in range(width // MXU_DIM):
            col = sec * width + c * MXU_DIM
            xc = p[:, c * MXU_DIM:(c + 1) * MXU_DIM]
            y = xc * lax.rsqrt(_group_mean_sq(xc, r_ref) + EPS) * gqk_ref[:, col:col + MXU_DIM]
            for t in range(MXU_DIM // LANES):
                yt = y[:, t * LANES:(t + 1) * LANES]
                partner = jnp.where(first_half, pltpu.roll(yt, LANES - ROPE_HALF, 1), pltpu.roll(yt, ROPE_HALF, 1))
                out_col = c * MXU_DIM + t * LANES
                o_ref[0, :, out_col:out_col + LANES] = (yt * rope_c + partner * rope_s).astype(BF16)


def _in_proj_call(x, mod_l, g, w, rmat, gqk, cos_t, sin_t):
    b, s, d = x.shape
    tm = IN_PROJ_TILE
    tok = lambda w: pl.BlockSpec((1, tm, w), lambda i, j: (i, j, 0))
    width = w.shape[1] // 6
    out = jax.ShapeDtypeStruct((b, s, width), BF16)
    return pl.pallas_call(
        _in_proj_kernel,
        grid=(b, s // tm),
        in_specs=[
            tok(d),
            pl.BlockSpec((1, 1, mod_l.shape[-1]), lambda i, j: (i, 0, 0)),
            _const_spec((1, d)),
            _const_spec(w.shape, True), _const_spec(rmat.shape), _const_spec(gqk.shape),
            tok(LANES), tok(LANES),
        ],
        out_specs=[tok(width)] * 6,
        out_shape=[out] * 6,
        compiler_params=_params(("parallel", "parallel")),
        name="in_proj",
    )(x, mod_l, g, w, rmat, gqk, cos_t, sin_t)


def _diff_attn_kernel(bound_ref, q_ref, k_ref, v_ref, lam_ref, g_ref, o_ref, acc_ref):
    tq = q_ref.shape[1]
    seq = k_ref.shape[1]
    heads = q_ref.shape[2] // LANES
    qi = pl.program_id(2)
    lanes = lambda hh: slice(hh * LANES, (hh + 1) * LANES)

    def stacked_q(hh):
        q = q_ref[0, :, lanes(hh)]
        lane = lax.broadcasted_iota(jnp.int32, q.shape, 1)
        zero = jnp.zeros_like(q)
        return jnp.concatenate([jnp.where(lane < HEAD_DIM, q, zero), jnp.where(lane >= HEAD_DIM, q, zero)], axis=0)

    def causal(s, k0):
        qry = lax.broadcasted_iota(jnp.int32, s.shape, 0) % tq
        key = lax.broadcasted_iota(jnp.int32, s.shape, 1) + k0
        return jnp.where(key <= qry, s, NEG_INF)

    def v_ext(hh, k0, n):
        return jnp.concatenate([v_ref[0, pl.ds(k0, n), lanes(hh)], jnp.ones((n, LANES), BF16)], axis=1)

    def unshifted():
        for blk in range(seq // tq):
            @pl.when(qi == blk)
            def _(blk=blk):
                kv = (blk + 1) * tq
                for hh in range(heads):
                    s = causal(_dot_nt(stacked_q(hh), k_ref[0, :kv, lanes(hh)]), -blk * tq)
                    acc_ref[hh] = _dot(jnp.exp2(s).astype(BF16), v_ext(hh, 0, kv))

    def shifted():
        for hh in range(heads):
            qb = stacked_q(hh)

            def step(j, carry, diagonal, hh=hh, qb=qb):
                m, acc = carry
                k0 = pl.multiple_of(j * tq, tq)
                s = _dot_nt(qb, k_ref[0, pl.ds(k0, tq), lanes(hh)])
                if diagonal:
                    s = causal(s, 0)
                m_new = jnp.maximum(m, jnp.max(s, axis=-1, keepdims=True))
                acc = jnp.exp2(m - m_new) * acc + _dot(jnp.exp2(s - m_new).astype(BF16), v_ext(hh, k0, tq))
                return m_new, acc

            carry = (jnp.full((2 * tq, 1), NEG_INF, F32), jnp.zeros((2 * tq, 2 * LANES), F32))
            carry = lax.fori_loop(0, qi, lambda j, c, step=step: step(j, c, False), carry)
            acc_ref[hh] = step(qi, carry, True)[1]

    safe = bound_ref[0] <= SAFE_LOGIT
    pl.when(safe)(unshifted)
    pl.when(jnp.logical_not(safe))(shifted)

    for hh in range(heads):
        acc = acc_ref[hh]
        o1 = acc[:tq, :LANES] / acc[:tq, LANES:]
        o2 = acc[tq:, :LANES] / acc[tq:, LANES:]
        o = o1 - lam_ref[0:1, 0:1] * o2
        ms = jnp.mean(o * o, axis=-1, keepdims=True)
        o_ref[0, :, lanes(hh)] = (o * lax.rsqrt(ms + EPS) * g_ref[...]).astype(o_ref.dtype)


def _diff_attn_call(bound, dq, dk, dv, lam_l, g_sub):
    b, s, w = dq.shape
    tq = TOKEN_TILE
    cols = DIFF_HEADS_PER_STEP * LANES
    return pl.pallas_call(
        _diff_attn_kernel,
        grid=(b, w // cols, s // tq),
        in_specs=[
            _SMEM_SPEC,
            pl.BlockSpec((1, tq, cols), lambda i, h, j: (i, j, h)),
            pl.BlockSpec((1, s, cols), lambda i, h, j: (i, 0, h)),
            pl.BlockSpec((1, s, cols), lambda i, h, j: (i, 0, h)),
            _const_spec((1, LANES)),
            _const_spec((1, LANES)),
        ],
        out_specs=pl.BlockSpec((1, tq, cols), lambda i, h, j: (i, j, h)),
        out_shape=jax.ShapeDtypeStruct((b, s, w), BF16),
        scratch_shapes=[pltpu.VMEM((DIFF_HEADS_PER_STEP, 2 * tq, 2 * LANES), F32)],
        compiler_params=_params(("parallel", "parallel", "arbitrary")),
        name="diff_attn",
    )(bound, dq, dk, dv, lam_l, g_sub)


def _rows(start, size, stride):
    return pl.ds(start, size, stride=stride) if stride > 1 else pl.ds(start, size)


def _dil_attn_kernel(bound_ref, q_ref, k_ref, v_ref, r_ref, g_ref, o_ref,
                     stage, stage_mid, qp, kp, vm, num, den, mx, num_mid, den_mid):
    seq = q_ref.shape[1]
    blk = DIL_BLOCK
    head0 = lax.broadcasted_iota(jnp.int32, (blk, LANES), 1) < HEAD_DIM
    strides = tuple(stride for _, stride in DILATED_PAIRS)
    mid = strides[1]
    mid_len = seq // mid

    def mid_rows(r, n_rows):
        return pl.ds((r % mid) * mid_len + r // mid, n_rows, stride=mid)

    def stage_rows(di, dst_rows, q, k, v):
        lane0 = lax.broadcasted_iota(jnp.int32, v.shape, 1) < HEAD_DIM
        one = jnp.ones_like(v)
        if di > 0:
            qp[di - 1, dst_rows, :] = q
            kp[di - 1, dst_rows, :] = k
        vm[di, 0, dst_rows, :] = jnp.where(lane0, v, one)
        vm[di, 1, dst_rows, :] = jnp.where(lane0, one, v)

    stage_rows(0, slice(None), None, None, v_ref[0])
    for j, ref in enumerate((q_ref, k_ref, v_ref)):
        stage[j] = ref[0].astype(F32)
        for r in range(mid):
            stage_mid[j, r * mid_len:(r + 1) * mid_len, :] = stage[j, pl.ds(r, mid_len, stride=mid), :]
    stage_rows(1, slice(None), *(stage_mid[j].astype(BF16) for j in range(3)))
    last_len = seq // strides[2]
    for r in range(strides[2]):
        stage_rows(2, slice(r * last_len, (r + 1) * last_len),
                   *(stage_mid[j, mid_rows(r, last_len), :].astype(BF16) for j in range(3)))

    def q_rows(di, rows):
        return q_ref[0, rows, :] if di == 0 else qp[di - 1, rows, :]

    def k_rows(di, rows):
        return k_ref[0, rows, :] if di == 0 else kp[di - 1, rows, :]

    def window(ctx):
        qq = lax.broadcasted_iota(jnp.int32, (2 * blk, ctx), 0) % blk
        kk = lax.broadcasted_iota(jnp.int32, (2 * blk, ctx), 1)
        return (kk <= qq) if ctx == blk else ((kk >= qq) & (kk <= qq + blk))

    def block(di, r, n, mode):
        stride = strides[di]
        first = di == 0
        q0 = r * (seq // stride) + n * blk
        k0 = q0 - blk if n > 0 else q0
        if not isinstance(r, int):
            q0, k0 = pl.multiple_of(q0, blk), pl.multiple_of(k0, blk)
        ctx = 2 * blk if n > 0 else blk
        out_rows = _rows(r + n * blk * stride, blk, stride)
        q = q_rows(di, pl.ds(q0, blk))
        zero = jnp.zeros_like(q)
        qb = jnp.concatenate([jnp.where(head0, q, zero), jnp.where(head0, zero, q)], axis=0)
        s = _dot_nt(qb, k_rows(di, pl.ds(k0, ctx)))
        s = jnp.where(window(ctx), s, NEG_INF)
        if mode == "max":
            m = jnp.max(s, axis=-1, keepdims=True)
            m = jnp.where(head0, m[:blk], m[blk:])
            mx[out_rows, :] = m if first else jnp.maximum(mx[out_rows, :], m)
            return
        if mode == "shifted":
            m = mx[out_rows, :]
            s = s - jnp.concatenate([m[:, 0:1], m[:, HEAD_DIM:HEAD_DIM + 1]], axis=0)
        p = jnp.exp2(s).astype(BF16)
        out0 = _dot(p[:blk], vm[di, 0, pl.ds(k0, ctx), :])
        out1 = _dot(p[blk:], vm[di, 1, pl.ds(k0, ctx), :])
        n_blk = jnp.where(head0, out0, out1)
        d_blk = jnp.where(head0, out1, out0)
        if di == 0:
            num[out_rows, :] = n_blk
            den[out_rows, :] = d_blk
        elif di == 2:
            num_mid[mid_rows(r, blk), :] = n_blk
            den_mid[mid_rows(r, blk), :] = d_blk
        else:
            num[out_rows, :] = num[out_rows, :] + (n_blk + num_mid[pl.ds(q0, blk), :])
            den[out_rows, :] = den[out_rows, :] + (d_blk + den_mid[pl.ds(q0, blk), :])

    def sweep(mode, straight_line):
        for di in (0, 2, 1):
            stride = strides[di]
            nb = seq // stride // blk
            if straight_line or stride == 1:
                for r in range(stride):
                    for n in range(nb):
                        block(di, r, n, mode)
            else:
                def subsequence(r, _, di=di, nb=nb):
                    for n in range(nb):
                        block(di, r, n, mode)
                    return 0
                lax.fori_loop(0, stride, subsequence, 0)

    safe = bound_ref[0] <= SAFE_LOGIT

    @pl.when(safe)
    def _():
        sweep("plain", True)

    @pl.when(jnp.logical_not(safe))
    def _():
        sweep("max", False)
        sweep("shifted", False)

    o = num[...] / pltpu.roll(den[...], HEAD_DIM, 1)
    ms = _dot((o * o).astype(BF16), r_ref[...])
    o_ref[0] = (o * lax.rsqrt(ms + EPS) * g_ref[...]).astype(o_ref.dtype)


def _dil_attn_call(bound, lq, lk, lv, rmat, gd):
    b, s, w = lq.shape
    spec = pl.BlockSpec((1, s, LANES), lambda i, h: (i, 0, h))
    f32_rows = pltpu.VMEM((s, LANES), F32)
    npairs = len(DILATED_PAIRS)
    strides = tuple(stride for _, stride in DILATED_PAIRS)
    assert strides == (1, strides[1], strides[1] ** 2) and s // strides[2] == DIL_BLOCK
    assert all(window // stride == DIL_BLOCK for window, stride in DILATED_PAIRS)
    return pl.pallas_call(
        _dil_attn_kernel,
        grid=(b, w // LANES),
        in_specs=[_SMEM_SPEC, spec, spec, spec, _const_spec((LANES, LANES)),
                  pl.BlockSpec((1, LANES), lambda i, h: (0, h))],
        out_specs=spec,
        out_shape=jax.ShapeDtypeStruct((b, s, w), BF16),
        scratch_shapes=[pltpu.VMEM((3, s, LANES), F32),
                        pltpu.VMEM((3, s, LANES), F32),
                        pltpu.VMEM((npairs - 1, s, LANES), BF16),
                        pltpu.VMEM((npairs - 1, s, LANES), BF16),
                        pltpu.VMEM((npairs, 2, s, LANES), BF16),
                        f32_rows, f32_rows, f32_rows, f32_rows, f32_rows],
        compiler_params=_params(("parallel", "parallel")),
        name="dil_attn",
    )(bound, lq, lk, lv, rmat, gd)


def _mix_ffn_kernel(x_ref, mod_ref, oa_ref, ob_ref, wo_ref, g_ref, wu_ref, cw_ref, cb_ref, wd_ref,
                    xo_ref, tail_ref):
    d = x_ref.shape[-1]
    tm = x_ref.shape[1]
    d_ff = wd_ref.shape[0]
    nchunk = d_ff // FF_CHUNK
    halo = tail_ref.shape[2]
    wa = oa_ref.shape[-1]

    @pl.when(pl.program_id(1) == 0)
    def _():
        tail_ref[...] = jnp.zeros_like(tail_ref)

    mod = mod_ref[0]
    mixed = _dot(oa_ref[0], wo_ref[:wa, :]) + _dot(ob_ref[0], wo_ref[wa:, :])
    x = x_ref[0] + mod[:, 2 * d:3 * d] * mixed
    h = _modulated_norm(x, g_ref[...], mod[:, 3 * d:4 * d], mod[:, 4 * d:5 * d]).astype(BF16)
    row = lax.broadcasted_iota(jnp.int32, (halo, FF_CHUNK), 0)

    def chunk(c):
        convs = []
        for part in range(2):
            cols = slice(part * d_ff + c * FF_CHUNK, part * d_ff + (c + 1) * FF_CHUNK)
            u = _dot(h, wu_ref[:, cols])
            prev = tail_ref[c, part]
            tail_ref[c, part] = u[tm - halo:, :]
            conv = cb_ref[:, cols] + u * cw_ref[CONV_WIDTH - 1:CONV_WIDTH, cols]
            for lag in range(1, CONV_WIDTH):
                shifted = pltpu.roll(u, lag, 0)
                head = jnp.where(row < lag, pltpu.roll(prev, lag, 0), shifted[:halo])
                shifted = jnp.concatenate([head, shifted[halo:]], axis=0)
                conv = conv + shifted * cw_ref[CONV_WIDTH - 1 - lag:CONV_WIDTH - lag, cols]
            convs.append(conv)
        half_gate, val = convs
        return ((half_gate + half_gate * jnp.tanh(half_gate)) * val).astype(BF16)

    y = None
    for c0 in range(0, nchunk, FF_DOWN_GROUP):
        c1 = min(c0 + FF_DOWN_GROUP, nchunk)
        act = jnp.concatenate([chunk(c) for c in range(c0, c1)], axis=1)
        part = _dot(act, wd_ref[c0 * FF_CHUNK:c1 * FF_CHUNK, :])
        y = part if y is None else y + part
    xo_ref[0] = x + mod[:, 5 * d:6 * d] * y


def _mix_ffn_call(x, mod_l, oa, ob, wo, g, wu, cw, cb, wd):
    b, s, d = x.shape
    tm = TOKEN_TILE
    tok = lambda w: pl.BlockSpec((1, tm, w), lambda i, j: (i, j, 0))
    nchunk = wd.shape[0] // FF_CHUNK
    return pl.pallas_call(
        _mix_ffn_kernel,
        grid=(b, s // tm),
        in_specs=[tok(d), pl.BlockSpec((1, 1, mod_l.shape[-1]), lambda i, j: (i, 0, 0)),
                  tok(oa.shape[-1]), tok(ob.shape[-1]), _const_spec(wo.shape, True), _const_spec((1, d)),
                  _const_spec(wu.shape, True), _const_spec(cw.shape), _const_spec(cb.shape),
                  _const_spec(wd.shape, True)],
        out_specs=tok(d),
        out_shape=jax.ShapeDtypeStruct(x.shape, x.dtype),
        scratch_shapes=[pltpu.VMEM((nchunk, 2, SUBLANES, FF_CHUNK), F32)],
        compiler_params=_params(("parallel", "arbitrary")),
        name="mix_ffn",
    )(x, mod_l, oa, ob, wo, g, wu, cw, cb, wd)


def _group_mean_matrix(n):
    idx = np.arange(n) // HEAD_DIM
    return jnp.asarray((idx[:, None] == idx[None, :]).astype(np.float32) / HEAD_DIM, dtype=BF16)


def _logit_bound(g_q, g_k):
    return (HEAD_DIM * QK_SCALE * jnp.max(jnp.abs(g_q)) * jnp.max(jnp.abs(g_k))).reshape(1).astype(F32)


def kernel(x, c, positions, g_mix, g_ffn, w_ada, b_ada, w_in, w_out, diff_q_g, diff_k_g, lam_q1, lam_k1, lam_q2, lam_k2, diff_subln_g, dil_q_g, dil_k_g, dil_out_g, w_up, conv_w, conv_b, w_down):
    depth, d, _ = w_in.shape
    b = x.shape[0]
    d_ff = w_down.shape[1]

    lam_init = np.array([0.8 - 0.6 * math.exp(-0.3 * l) for l in range(depth)], np.float32)
    lam_init_tile = jnp.asarray(np.broadcast_to(lam_init[:, None, None], (depth, 1, LANES)))
    mod, lam = _mod_call(c, w_ada, b_ada, lam_q1, lam_k1, lam_q2, lam_k2, lam_init_tile)
    cos_t, sin_t = _rope_call(positions)
    rmat_qk = _group_mean_matrix(MXU_DIM)
    rmat_out = _group_mean_matrix(LANES)
    gate_half = jnp.concatenate([jnp.full((1, d_ff), 0.5, F32), jnp.ones((1, d_ff), F32)], axis=1)
    reps = DIFF_WIDTH // HEAD_DIM
    ones = jnp.ones((DIFF_WIDTH,), F32)

    for l in range(depth):
        gqk = jnp.concatenate([jnp.tile(diff_q_g[l], reps) * QK_SCALE, jnp.tile(diff_k_g[l], reps), ones,
                               jnp.tile(dil_q_g[l], reps) * QK_SCALE, jnp.tile(dil_k_g[l], reps), ones]).reshape(1, -1)
        mod_l = mod[l].reshape(b, 1, -1)
        dq, dk, dv, lq, lk, lv = _in_proj_call(x, mod_l, g_mix[l].reshape(1, d), w_in[l].astype(BF16), rmat_qk, gqk,
                                               cos_t, sin_t)
        g_sub = (diff_subln_g[l] * (1.0 - float(lam_init[l]))).reshape(1, DIFF_V_DIM)
        oa = _diff_attn_call(_logit_bound(diff_q_g[l], diff_k_g[l]), dq, dk, dv, lam[l], g_sub)
        gd = jnp.tile(dil_out_g[l], N_DIL_HEADS).reshape(1, DIL_WIDTH)
        ob = _dil_attn_call(_logit_bound(dil_q_g[l], dil_k_g[l]), lq, lk, lv, rmat_out, gd)
        x = _mix_ffn_call(x, mod_l, oa, ob, w_out[l].astype(BF16), g_ffn[l].reshape(1, d), w_up[l].astype(BF16),
                          conv_w[l] * gate_half, conv_b[l].reshape(1, -1) * gate_half, w_down[l].astype(BF16))
    return x
```

```python
import math

import jax
import jax.numpy as jnp
import numpy as np
from jax import lax
from jax.experimental import pallas as pl
from jax.experimental.pallas import tpu as pltpu

HEAD_DIM = 64
N_DIFF_HEADS = 4
DIFF_V_DIM = 2 * HEAD_DIM
DIFF_WIDTH = N_DIFF_HEADS * DIFF_V_DIM
N_DIL_HEADS = 8
DIL_WIDTH = N_DIL_HEADS * HEAD_DIM
DILATED_PAIRS = ((128, 1), (512, 4), (2048, 16))
ROPE_THETA = 500000.0
ROPE_DIM = HEAD_DIM // 4
ROPE_HALF = ROPE_DIM // 2
CONV_WIDTH = 3
EPS = 1e-6
NEG_INF = -1e30
LOG2E = 1.4426950408889634
QK_SCALE = HEAD_DIM ** -0.5 * LOG2E

LANES = 128
SUBLANES = 8
MXU_DIM = 256
TOKEN_TILE = 256
IN_PROJ_TILE = 512
DIFF_HEADS_PER_STEP = 4
DIL_BLOCK = 128
FF_CHUNK = 256
FF_DOWN_GROUP = 6
SAFE_LOGIT = 64.0
VMEM_LIMIT = 56 * 1024 * 1024

F32 = jnp.float32
BF16 = jnp.bfloat16


def _dot(a, b):
    return jnp.dot(a, b, preferred_element_type=F32)


def _dot_nt(a, b):
    return lax.dot_general(a, b, (((1,), (1,)), ((), ())), preferred_element_type=F32)


def _split_bf16(x):
    hi = x.astype(BF16)
    lo = (x - hi.astype(F32)).astype(BF16)
    return hi, lo


def _params(sem, vmem=VMEM_LIMIT):
    return pltpu.CompilerParams(dimension_semantics=sem, vmem_limit_bytes=vmem)


def _const_spec(shape, single_buffer=False):
    mode = pl.Buffered(1) if single_buffer else None
    return pl.BlockSpec(shape, lambda *_: (0,) * len(shape), pipeline_mode=mode)


_SMEM_SPEC = pl.BlockSpec(memory_space=pltpu.SMEM)


def _mod_kernel(c_ref, w_ref, b_ref, q1_ref, k1_ref, q2_ref, k2_ref, li_ref, mod_ref, lam_ref):
    c = c_ref[...]
    cond = c * (1.0 / (1.0 + jnp.exp(-c)))
    ch, cl = _split_bf16(cond)
    wh, wl = _split_bf16(w_ref[0])
    mod_ref[0] = _dot(ch, wh) + (_dot(ch, wl) + _dot(cl, wh)) + b_ref[0]
    s1 = jnp.sum(q1_ref[0] * k1_ref[0], axis=-1, keepdims=True)
    s2 = jnp.sum(q2_ref[0] * k2_ref[0], axis=-1, keepdims=True)
    lam_ref[0] = (jnp.exp(s1) - jnp.exp(s2)) + li_ref[0]


def _mod_call(c, w_ada, b_ada, lam_q1, lam_k1, lam_q2, lam_k2, lam_init):
    depth, d, d6 = w_ada.shape
    b = c.shape[0]
    nj = d6 // d
    vec = lambda a: a.reshape(depth, 1, HEAD_DIM)
    vspec = pl.BlockSpec((1, 1, HEAD_DIM), lambda l, j: (l, 0, 0))
    return pl.pallas_call(
        _mod_kernel,
        grid=(depth, nj),
        in_specs=[
            pl.BlockSpec((b, d), lambda l, j: (0, 0)),
            pl.BlockSpec((1, d, d), lambda l, j: (l, 0, j)),
            pl.BlockSpec((1, 1, d), lambda l, j: (l, 0, j)),
            vspec, vspec, vspec, vspec,
            pl.BlockSpec((1, 1, LANES), lambda l, j: (l, 0, 0)),
        ],
        out_specs=[
            pl.BlockSpec((1, b, d), lambda l, j: (l, 0, j)),
            pl.BlockSpec((1, 1, LANES), lambda l, j: (l, 0, 0)),
        ],
        out_shape=[
            jax.ShapeDtypeStruct((depth, b, d6), F32),
            jax.ShapeDtypeStruct((depth, 1, LANES), F32),
        ],
        compiler_params=_params(("arbitrary", "arbitrary")),
        name="adaln_mod",
    )(c, w_ada, b_ada.reshape(depth, 1, d6), vec(lam_q1), vec(lam_k1), vec(lam_q2), vec(lam_k2), lam_init)


def _rope_kernel(pos_ref, invf_ref, c_ref, s_ref):
    ang = pos_ref[0].astype(F32) * invf_ref[...]
    c_ref[0] = jnp.cos(ang)
    s_ref[0] = jnp.sin(ang)


def _rope_call(positions):
    b, s = positions.shape
    per_row = LANES // ROPE_HALF
    rows = s // per_row
    inv_freq = ROPE_THETA ** (-jnp.arange(0, ROPE_DIM, 2, dtype=F32) / ROPE_DIM)
    invf = jnp.tile(inv_freq, per_row).reshape(1, LANES)
    pos = jnp.repeat(positions.reshape(b, rows, per_row), ROPE_HALF, axis=-1)
    spec = pl.BlockSpec((1, rows, LANES), lambda i: (i, 0, 0))
    out = jax.ShapeDtypeStruct((b, rows, LANES), F32)
    cos, sin = pl.pallas_call(
        _rope_kernel,
        grid=(b,),
        in_specs=[spec, _const_spec((1, LANES))],
        out_specs=[spec, spec],
        out_shape=[out, out],
        compiler_params=_params(("parallel",)),
        name="rope_tables",
    )(pos, invf)
    lane_tile = lambda t: jnp.tile(t.reshape(b, s, ROPE_HALF), (1, 1, per_row))
    return lane_tile(cos), lane_tile(sin)


def _modulated_norm(x, g, shift, scale):
    ms = jnp.mean(x * x, axis=-1, keepdims=True)
    return (x * lax.rsqrt(ms + EPS) * g) * (1.0 + scale) + shift


def _group_mean_sq(y, r_ref):
    return _dot((y * y).astype(BF16), r_ref[...])


def _in_proj_kernel(x_ref, mod_ref, g_ref, w_ref, r_ref, gqk_ref, cos_ref, sin_ref,
                    dq_ref, dk_ref, dv_ref, lq_ref, lk_ref, lv_ref):
    d = x_ref.shape[-1]
    mod = mod_ref[0]
    h = _modulated_norm(x_ref[0], g_ref[...], mod[:, 0:d], mod[:, d:2 * d]).astype(BF16)
    dim = lax.broadcasted_iota(jnp.int32, cos_ref.shape[1:], 1) % HEAD_DIM
    first_half = dim < ROPE_HALF
    rope_c = jnp.where(dim < ROPE_DIM, cos_ref[0], 1.0)
    rope_s = jnp.where(first_half, -sin_ref[0], jnp.where(dim < ROPE_DIM, sin_ref[0], 0.0))
    width = dq_ref.shape[-1]
    for sec, o_ref in enumerate((dq_ref, dk_ref, dv_ref, lq_ref, lk_ref, lv_ref)):
        p = _dot(h, w_ref[:, sec * width:(sec + 1) * width])
        if o_ref is dv_ref or o_ref is lv_ref:
            o_ref[0] = p.astype(BF16)
            continue
        for c in range(width // MXU_DIM):
            col = sec * width + c * MXU_DIM
            xc = p[:, c * MXU_DIM:(c + 1) * MXU_DIM]
            y = xc * lax.rsqrt(_group_mean_sq(xc, r_ref) + EPS) * gqk_ref[:, col:col + MXU_DIM]
            for t in range(MXU_DIM // LANES):
                yt = y[:, t * LANES:(t + 1) * LANES]
                partner = jnp.where(first_half, pltpu.roll(yt, LANES - ROPE_HALF, 1), pltpu.roll(yt, ROPE_HALF, 1))
                out_col = c * MXU_DIM + t * LANES
                o_ref[0, :, out_col:out_col + LANES] = (yt * rope_c + partner * rope_s).astype(BF16)


def _in_proj_call(x, mod_l, g, w, rmat, gqk, cos_t, sin_t):
    b, s, d = x.shape
    tm = IN_PROJ_TILE
    tok = lambda w: pl.BlockSpec((1, tm, w), lambda i, j: (i, j, 0))
    width = w.shape[1] // 6
    out = jax.ShapeDtypeStruct((b, s, width), BF16)
    return pl.pallas_call(
        _in_proj_kernel,
        grid=(b, s // tm),
        in_specs=[
            tok(d),
            pl.BlockSpec((1, 1, mod_l.shape[-1]), lambda i, j: (i, 0, 0)),
            _const_spec((1, d)),
            _const_spec(w.shape, True), _const_spec(rmat.shape), _const_spec(gqk.shape),
            tok(LANES), tok(LANES),
        ],
        out_specs=[tok(width)] * 6,
        out_shape=[out] * 6,
        compiler_params=_params(("parallel", "parallel")),
        name="in_proj",
    )(x, mod_l, g, w, rmat, gqk, cos_t, sin_t)


def _diff_attn_kernel(bound_ref, q_ref, k_ref, v_ref, lam_ref, g_ref, o_ref):
    tq = q_ref.shape[1]
    seq = k_ref.shape[1]
    heads = q_ref.shape[2] // LANES
    qi = pl.program_id(2)
    lanes = lambda hh: slice(hh * LANES, (hh + 1) * LANES)

    def stacked_q(hh):
        q = q_ref[0, :, lanes(hh)]
        lane = lax.broadcasted_iota(jnp.int32, q.shape, 1)
        zero = jnp.zeros_like(q)
        return jnp.concatenate([jnp.where(lane < HEAD_DIM, q, zero), jnp.where(lane >= HEAD_DIM, q, zero)], axis=0)

    def causal(s, k0):
        qry = lax.broadcasted_iota(jnp.int32, s.shape, 0) % tq
        key = lax.broadcasted_iota(jnp.int32, s.shape, 1) + k0
        return jnp.where(key <= qry, s, NEG_INF)

    def v_ext(hh, k0, n):
        return jnp.concatenate([v_ref[0, pl.ds(k0, n), lanes(hh)], jnp.ones((n, LANES), BF16)], axis=1)

    def finish(hh, acc):
        o1 = acc[:tq, :LANES] / acc[:tq, LANES:]
        o2 = acc[tq:, :LANES] / acc[tq:, LANES:]
        o = o1 - lam_ref[0:1, 0:1] * o2
        ms = jnp.mean(o * o, axis=-1, keepdims=True)
        o_ref[0, :, lanes(hh)] = (o * lax.rsqrt(ms + EPS) * g_ref[...]).astype(o_ref.dtype)

    def unshifted():
        for blk in range(seq // tq):
            @pl.when(qi == blk)
            def _(blk=blk):
                kv = (blk + 1) * tq
                for hh in range(heads):
                    s = causal(_dot_nt(stacked_q(hh), k_ref[0, :kv, lanes(hh)]), -blk * tq)
                    finish(hh, _dot(jnp.exp2(s).astype(BF16), v_ext(hh, 0, kv)))

    def shifted():
        for hh in range(heads):
            qb = stacked_q(hh)

            def step(j, carry, diagonal, hh=hh, qb=qb):
                m, acc = carry
                k0 = pl.multiple_of(j * tq, tq)
                s = _dot_nt(qb, k_ref[0, pl.ds(k0, tq), lanes(hh)])
                if diagonal:
                    s = causal(s, 0)
                m_new = jnp.maximum(m, jnp.max(s, axis=-1, keepdims=True))
                acc = jnp.exp2(m - m_new) * acc + _dot(jnp.exp2(s - m_new).astype(BF16), v_ext(hh, k0, tq))
                return m_new, acc

            carry = (jnp.full((2 * tq, 1), NEG_INF, F32), jnp.zeros((2 * tq, 2 * LANES), F32))
            carry = lax.fori_loop(0, qi, lambda j, c, step=step: step(j, c, False), carry)
            finish(hh, step(qi, carry, True)[1])

    safe = bound_ref[0] <= SAFE_LOGIT
    pl.when(safe)(unshifted)
    pl.when(jnp.logical_not(safe))(shifted)


def _diff_attn_call(bound, dq, dk, dv, lam_l, g_sub):
    b, s, w = dq.shape
    tq = TOKEN_TILE
    cols = DIFF_HEADS_PER_STEP * LANES
    return pl.pallas_call(
        _diff_attn_kernel,
        grid=(b, w // cols, s // tq),
        in_specs=[
            _SMEM_SPEC,
            pl.BlockSpec((1, tq, cols), lambda i, h, j: (i, j, h)),
            pl.BlockSpec((1, s, cols), lambda i, h, j: (i, 0, h)),
            pl.BlockSpec((1, s, cols), lambda i, h, j: (i, 0, h)),
            _const_spec((1, LANES)),
            _const_spec((1, LANES)),
        ],
        out_specs=pl.BlockSpec((1, tq, cols), lambda i, h, j: (i, j, h)),
        out_shape=jax.ShapeDtypeStruct((b, s, w), BF16),
        compiler_params=_params(("parallel", "parallel", "arbitrary")),
        name="diff_attn",
    )(bound, dq, dk, dv, lam_l, g_sub)


def _rows(start, size, stride):
    return pl.ds(start, size, stride=stride) if stride > 1 else pl.ds(start, size)


def _dil_attn_kernel(bound_ref, q_ref, k_ref, v_ref, r_ref, g_ref, o_ref,
                     stage, stage_mid, qp, kp, vm, num, den, mx, num_mid, den_mid):
    seq = q_ref.shape[1]
    blk = DIL_BLOCK
    head0 = lax.broadcasted_iota(jnp.int32, (blk, LANES), 1) < HEAD_DIM
    strides = tuple(stride for _, stride in DILATED_PAIRS)
    mid = strides[1]
    mid_len = seq // mid

    def mid_rows(r, n_rows):
        return pl.ds((r % mid) * mid_len + r // mid, n_rows, stride=mid)

    def stage_rows(di, dst_rows, q, k, v):
        lane0 = lax.broadcasted_iota(jnp.int32, v.shape, 1) < HEAD_DIM
        one = jnp.ones_like(v)
        if di > 0:
            qp[di - 1, dst_rows, :] = q
            kp[di - 1, dst_rows, :] = k
        vm[di, 0, dst_rows, :] = jnp.where(lane0, v, one)
        vm[di, 1, dst_rows, :] = jnp.where(lane0, one, v)

    stage_rows(0, slice(None), None, None, v_ref[0])
    for j, ref in enumerate((q_ref, k_ref, v_ref)):
        stage[j] = ref[0].astype(F32)
        for r in range(mid):
            stage_mid[j, r * mid_len:(r + 1) * mid_len, :] = stage[j, pl.ds(r, mid_len, stride=mid), :]
    stage_rows(1, slice(None), *(stage_mid[j].astype(BF16) for j in range(3)))
    last_len = seq // strides[2]
    for r in range(strides[2]):
        stage_rows(2, slice(r * last_len, (r + 1) * last_len),
                   *(stage_mid[j, mid_rows(r, last_len), :].astype(BF16) for j in range(3)))

    def q_rows(di, rows):
        return q_ref[0, rows, :] if di == 0 else qp[di - 1, rows, :]

    def k_rows(di, rows):
        return k_ref[0, rows, :] if di == 0 else kp[di - 1, rows, :]

    def window(ctx):
        qq = lax.broadcasted_iota(jnp.int32, (2 * blk, ctx), 0) % blk
        kk = lax.broadcasted_iota(jnp.int32, (2 * blk, ctx), 1)
        return (kk <= qq) if ctx == blk else ((kk >= qq) & (kk <= qq + blk))

    def block(di, r, n, mode):
        stride = strides[di]
        first = di == 0
        q0 = r * (seq // stride) + n * blk
        k0 = q0 - blk if n > 0 else q0
        if not isinstance(r, int):
            q0, k0 = pl.multiple_of(q0, blk), pl.multiple_of(k0, blk)
        ctx = 2 * blk if n > 0 else blk
        out_rows = _rows(r + n * blk * stride, blk, stride)
        q = q_rows(di, pl.ds(q0, blk))
        zero = jnp.zeros_like(q)
        qb = jnp.concatenate([jnp.where(head0, q, zero), jnp.where(head0, zero, q)], axis=0)
        s = _dot_nt(qb, k_rows(di, pl.ds(k0, ctx)))
        s = jnp.where(window(ctx), s, NEG_INF)
        if mode == "max":
            m = jnp.max(s, axis=-1, keepdims=True)
            m = jnp.where(head0, m[:blk], m[blk:])
            mx[out_rows, :] = m if first else jnp.maximum(mx[out_rows, :], m)
            return
        if mode == "shifted":
            m = mx[out_rows, :]
            s = s - jnp.concatenate([m[:, 0:1], m[:, HEAD_DIM:HEAD_DIM + 1]], axis=0)
        p = jnp.exp2(s).astype(BF16)
        out0 = _dot(p[:blk], vm[di, 0, pl.ds(k0, ctx), :])
        out1 = _dot(p[blk:], vm[di, 1, pl.ds(k0, ctx), :])
        n_blk = jnp.where(head0, out0, out1)
        d_blk = jnp.where(head0, out1, out0)
        if di == 0:
            num[out_rows, :] = n_blk
            den[out_rows, :] = d_blk
        elif di == 2:
            num_mid[mid_rows(r, blk), :] = n_blk
            den_mid[mid_rows(r, blk), :] = d_blk
        else:
            num[out_rows, :] = num[out_rows, :] + (n_blk + num_mid[pl.ds(q0, blk), :])
            den[out_rows, :] = den[out_rows, :] + (d_blk + den_mid[pl.ds(q0, blk), :])

    def sweep(mode, straight_line):
        for di in (0, 2, 1):
            stride = strides[di]
            nb = seq // stride // blk
            if straight_line or stride == 1:
                for r in range(stride):
                    for n in range(nb):
                        block(di, r, n, mode)
            else:
                def subsequence(r, _, di=di, nb=nb):
                    for n in range(nb):
                        block(di, r, n, mode)
                    return 0
                lax.fori_loop(0, stride, subsequence, 0)

    safe = bound_ref[0] <= SAFE_LOGIT

    @pl.when(safe)
    def _():
        sweep("plain", True)

    @pl.when(jnp.logical_not(safe))
    def _():
        sweep("max", False)
        sweep("shifted", False)

    o = num[...] / pltpu.roll(den[...], HEAD_DIM, 1)
    ms = _dot((o * o).astype(BF16), r_ref[...])
    o_ref[0] = (o * lax.rsqrt(ms + EPS) * g_ref[...]).astype(o_ref.dtype)


def _dil_attn_call(bound, lq, lk, lv, rmat, gd):
    b, s, w = lq.shape
    spec = pl.BlockSpec((1, s, LANES), lambda i, h: (i, 0, h))
    f32_rows = pltpu.VMEM((s, LANES), F32)
    npairs = len(DILATED_PAIRS)
    strides = tuple(stride for _, stride in DILATED_PAIRS)
    assert strides == (1, strides[1], strides[1] ** 2) and s // strides[2] == DIL_BLOCK
    assert all(window // stride == DIL_BLOCK for window, stride in DILATED_PAIRS)
    return pl.pallas_call(
        _dil_attn_kernel,
        grid=(b, w // LANES),
        in_specs=[_SMEM_SPEC, spec, spec, spec, _const_spec((LANES, LANES)),
                  pl.BlockSpec((1, LANES), lambda i, h: (0, h))],
        out_specs=spec,
        out_shape=jax.ShapeDtypeStruct((b, s, w), BF16),
        scratch_shapes=[pltpu.VMEM((3, s, LANES), F32),
                        pltpu.VMEM((3, s, LANES), F32),
                        pltpu.VMEM((npairs - 1, s, LANES), BF16),
                        pltpu.VMEM((npairs - 1, s, LANES), BF16),
                        pltpu.VMEM((npairs, 2, s, LANES), BF16),
                        f32_rows, f32_rows, f32_rows, f32_rows, f32_rows],
        compiler_params=_params(("parallel", "parallel")),
        name="dil_attn",
    )(bound, lq, lk, lv, rmat, gd)


def _mix_ffn_kernel(x_ref, mod_ref, oa_ref, ob_ref, wo_ref, g_ref, wu_ref, cw_ref, cb_ref, wd_ref,
                    xo_ref, tail_ref):
    d = x_ref.shape[-1]
    tm = x_ref.shape[1]
    d_ff = wd_ref.shape[0]
    nchunk = d_ff // FF_CHUNK
    halo = tail_ref.shape[2]
    wa = oa_ref.shape[-1]

    @pl.when(pl.program_id(1) == 0)
    def _():
        tail_ref[...] = jnp.zeros_like(tail_ref)

    mod = mod_ref[0]
    mixed = _dot(oa_ref[0], wo_ref[:wa, :]) + _dot(ob_ref[0], wo_ref[wa:, :])
    x = x_ref[0] + mod[:, 2 * d:3 * d] * mixed
    h = _modulated_norm(x, g_ref[...], mod[:, 3 * d:4 * d], mod[:, 4 * d:5 * d]).astype(BF16)
    row = lax.broadcasted_iota(jnp.int32, (halo, FF_CHUNK), 0)

    def chunk(c):
        convs = []
        for part in range(2):
            cols = slice(part * d_ff + c * FF_CHUNK, part * d_ff + (c + 1) * FF_CHUNK)
            u = _dot(h, wu_ref[:, cols])
            prev = tail_ref[c, part]
            tail_ref[c, part] = u[tm - halo:, :]
            conv = cb_ref[:, cols] + u * cw_ref[CONV_WIDTH - 1:CONV_WIDTH, cols]
            for lag in range(1, CONV_WIDTH):
                shifted = pltpu.roll(u, lag, 0)
                head = jnp.where(row < lag, pltpu.roll(prev, lag, 0), shifted[:halo])
                shifted = jnp.concatenate([head, shifted[halo:]], axis=0)
                conv = conv + shifted * cw_ref[CONV_WIDTH - 1 - lag:CONV_WIDTH - lag, cols]
            convs.append(conv)
        half_gate, val = convs
        return ((half_gate + half_gate * jnp.tanh(half_gate)) * val).astype(BF16)

    y = None
    for c0 in range(0, nchunk, FF_DOWN_GROUP):
        c1 = min(c0 + FF_DOWN_GROUP, nchunk)
        act = jnp.concatenate([chunk(c) for c in range(c0, c1)], axis=1)
        part = _dot(act, wd_ref[c0 * FF_CHUNK:c1 * FF_CHUNK, :])
        y = part if y is None else y + part
    xo_ref[0] = x + mod[:, 5 * d:6 * d] * y


def _mix_ffn_call(x, mod_l, oa, ob, wo, g, wu, cw, cb, wd):
    b, s, d = x.shape
    tm = TOKEN_TILE
    tok = lambda w: pl.BlockSpec((1, tm, w), lambda i, j: (i, j, 0))
    nchunk = wd.shape[0] // FF_CHUNK
    return pl.pallas_call(
        _mix_ffn_kernel,
        grid=(b, s // tm),
        in_specs=[tok(d), pl.BlockSpec((1, 1, mod_l.shape[-1]), lambda i, j: (i, 0, 0)),
                  tok(oa.shape[-1]), tok(ob.shape[-1]), _const_spec(wo.shape, True), _const_spec((1, d)),
                  _const_spec(wu.shape, True), _const_spec(cw.shape), _const_spec(cb.shape),
                  _const_spec(wd.shape, True)],
        out_specs=tok(d),
        out_shape=jax.ShapeDtypeStruct(x.shape, x.dtype),
        scratch_shapes=[pltpu.VMEM((nchunk, 2, SUBLANES, FF_CHUNK), F32)],
        compiler_params=_params(("parallel", "arbitrary")),
        name="mix_ffn",
    )(x, mod_l, oa, ob, wo, g, wu, cw, cb, wd)


def _group_mean_matrix(n):
    idx = np.arange(n) // HEAD_DIM
    return jnp.asarray((idx[:, None] == idx[None, :]).astype(np.float32) / HEAD_DIM, dtype=BF16)


def _logit_bound(g_q, g_k):
    return (HEAD_DIM * QK_SCALE * jnp.max(jnp.abs(g_q)) * jnp.max(jnp.abs(g_k))).reshape(1).astype(F32)


def kernel(x, c, positions, g_mix, g_ffn, w_ada, b_ada, w_in, w_out, diff_q_g, diff_k_g, lam_q1, lam_k1, lam_q2, lam_k2, diff_subln_g, dil_q_g, dil_k_g, dil_out_g, w_up, conv_w, conv_b, w_down):
    depth, d, _ = w_in.shape
    b = x.shape[0]
    d_ff = w_down.shape[1]

    lam_init = np.array([0.8 - 0.6 * math.exp(-0.3 * l) for l in range(depth)], np.float32)
    lam_init_tile = jnp.asarray(np.broadcast_to(lam_init[:, None, None], (depth, 1, LANES)))
    mod, lam = _mod_call(c, w_ada, b_ada, lam_q1, lam_k1, lam_q2, lam_k2, lam_init_tile)
    cos_t, sin_t = _rope_call(positions)
    rmat_qk = _group_mean_matrix(MXU_DIM)
    rmat_out = _group_mean_matrix(LANES)
    gate_half = jnp.concatenate([jnp.full((1, d_ff), 0.5, F32), jnp.ones((1, d_ff), F32)], axis=1)
    reps = DIFF_WIDTH // HEAD_DIM
    ones = jnp.ones((DIFF_WIDTH,), F32)

    for l in range(depth):
        gqk = jnp.concatenate([jnp.tile(diff_q_g[l], reps) * QK_SCALE, jnp.tile(diff_k_g[l], reps), ones,
                               jnp.tile(dil_q_g[l], reps) * QK_SCALE, jnp.tile(dil_k_g[l], reps), ones]).reshape(1, -1)
        mod_l = mod[l].reshape(b, 1, -1)
        dq, dk, dv, lq, lk, lv = _in_proj_call(x, mod_l, g_mix[l].reshape(1, d), w_in[l].astype(BF16), rmat_qk, gqk,
                                               cos_t, sin_t)
        g_sub = (diff_subln_g[l] * (1.0 - float(lam_init[l]))).reshape(1, DIFF_V_DIM)
        oa = _diff_attn_call(_logit_bound(diff_q_g[l], diff_k_g[l]), dq, dk, dv, lam[l], g_sub)
        gd = jnp.tile(dil_out_g[l], N_DIL_HEADS).reshape(1, DIL_WIDTH)
        ob = _dil_attn_call(_logit_bound(dil_q_g[l], dil_k_g[l]), lq, lk, lv, rmat_out, gd)
        x = _mix_ffn_call(x, mod_l, oa, ob, w_out[l].astype(BF16), g_ffn[l].reshape(1, d), w_up[l].astype(BF16),
                          conv_w[l] * gate_half, conv_b[l].reshape(1, -1) * gate_half, w_down[l].astype(BF16))
    return x
```

```python
import math

import jax
import jax.numpy as jnp
import numpy as np
from jax import lax
from jax.experimental import pallas as pl
from jax.experimental.pallas import tpu as pltpu

HEAD_DIM = 64
N_DIFF_HEADS = 4
DIFF_V_DIM = 2 * HEAD_DIM
DIFF_WIDTH = N_DIFF_HEADS * DIFF_V_DIM
N_DIL_HEADS = 8
DIL_WIDTH = N_DIL_HEADS * HEAD_DIM
DILATED_PAIRS = ((128, 1), (512, 4), (2048, 16))
ROPE_THETA = 500000.0
ROPE_DIM = HEAD_DIM // 4
ROPE_HALF = ROPE_DIM // 2
CONV_WIDTH = 3
EPS = 1e-6
NEG_INF = -1e30
LOG2E = 1.4426950408889634
QK_SCALE = HEAD_DIM ** -0.5 * LOG2E

LANES = 128
SUBLANES = 8
MXU_DIM = 256
DIFF_Q_BLOCK = 256
FFN_TILE = 512
IN_PROJ_TILE = 1024
DIFF_HEADS_PER_STEP = 4
DIL_BLOCK = 128
FF_CHUNK = 256
FF_DOWN_GROUP = 6
SAFE_LOGIT = 64.0
VMEM_LIMIT = 56 * 1024 * 1024

F32 = jnp.float32
BF16 = jnp.bfloat16


def _dot(a, b):
    return jnp.dot(a, b, preferred_element_type=F32)


def _dot_nt(a, b):
    return lax.dot_general(a, b, (((1,), (1,)), ((), ())), preferred_element_type=F32)


def _split_bf16(x):
    hi = x.astype(BF16)
    lo = (x - hi.astype(F32)).astype(BF16)
    return hi, lo


def _params(sem, vmem=VMEM_LIMIT):
    return pltpu.CompilerParams(dimension_semantics=sem, vmem_limit_bytes=vmem)


def _const_spec(shape, single_buffer=False):
    mode = pl.Buffered(1) if single_buffer else None
    return pl.BlockSpec(shape, lambda *_: (0,) * len(shape), pipeline_mode=mode)


_SMEM_SPEC = pl.BlockSpec(memory_space=pltpu.SMEM)


def _mod_kernel(c_ref, w_ref, b_ref, q1_ref, k1_ref, q2_ref, k2_ref, li_ref, mod_ref, lam_ref):
    c = c_ref[...]
    cond = c * (1.0 / (1.0 + jnp.exp(-c)))
    ch, cl = _split_bf16(cond)
    wh, wl = _split_bf16(w_ref[0])
    mod_ref[0] = _dot(ch, wh) + (_dot(ch, wl) + _dot(cl, wh)) + b_ref[0]
    s1 = jnp.sum(q1_ref[0] * k1_ref[0], axis=-1, keepdims=True)
    s2 = jnp.sum(q2_ref[0] * k2_ref[0], axis=-1, keepdims=True)
    lam_ref[0] = (jnp.exp(s1) - jnp.exp(s2)) + li_ref[0]


def _mod_call(c, w_ada, b_ada, lam_q1, lam_k1, lam_q2, lam_k2, lam_init):
    depth, d, d6 = w_ada.shape
    b = c.shape[0]
    nj = d6 // d
    vec = lambda a: a.reshape(depth, 1, HEAD_DIM)
    vspec = pl.BlockSpec((1, 1, HEAD_DIM), lambda l, j: (l, 0, 0))
    return pl.pallas_call(
        _mod_kernel,
        grid=(depth, nj),
        in_specs=[
            pl.BlockSpec((b, d), lambda l, j: (0, 0)),
            pl.BlockSpec((1, d, d), lambda l, j: (l, 0, j)),
            pl.BlockSpec((1, 1, d), lambda l, j: (l, 0, j)),
            vspec, vspec, vspec, vspec,
            pl.BlockSpec((1, 1, LANES), lambda l, j: (l, 0, 0)),
        ],
        out_specs=[
            pl.BlockSpec((1, b, d), lambda l, j: (l, 0, j)),
            pl.BlockSpec((1, 1, LANES), lambda l, j: (l, 0, 0)),
        ],
        out_shape=[
            jax.ShapeDtypeStruct((depth, b, d6), F32),
            jax.ShapeDtypeStruct((depth, 1, LANES), F32),
        ],
        compiler_params=_params(("arbitrary", "arbitrary")),
        name="adaln_mod",
    )(c, w_ada, b_ada.reshape(depth, 1, d6), vec(lam_q1), vec(lam_k1), vec(lam_q2), vec(lam_k2), lam_init)


def _rope_kernel(pos_ref, invf_ref, c_ref, s_ref):
    ang = pos_ref[0].astype(F32) * invf_ref[...]
    c_ref[0] = jnp.cos(ang)
    s_ref[0] = jnp.sin(ang)


def _rope_call(positions):
    b, s = positions.shape
    per_row = LANES // ROPE_HALF
    rows = s // per_row
    inv_freq = ROPE_THETA ** (-jnp.arange(0, ROPE_DIM, 2, dtype=F32) / ROPE_DIM)
    invf = jnp.tile(inv_freq, per_row).reshape(1, LANES)
    pos = jnp.repeat(positions.reshape(b, rows, per_row), ROPE_HALF, axis=-1)
    spec = pl.BlockSpec((1, rows, LANES), lambda i: (i, 0, 0))
    out = jax.ShapeDtypeStruct((b, rows, LANES), F32)
    cos, sin = pl.pallas_call(
        _rope_kernel,
        grid=(b,),
        in_specs=[spec, _const_spec((1, LANES))],
        out_specs=[spec, spec],
        out_shape=[out, out],
        compiler_params=_params(("parallel",)),
        name="rope_tables",
    )(pos, invf)
    lane_tile = lambda t: jnp.tile(t.reshape(b, s, ROPE_HALF), (1, 1, per_row))
    return lane_tile(cos), lane_tile(sin)


def _modulated_norm(x, g, shift, scale):
    ms = jnp.mean(x * x, axis=-1, keepdims=True)
    return (x * lax.rsqrt(ms + EPS) * g) * (1.0 + scale) + shift


def _group_mean_sq(y, r_ref):
    return _dot((y * y).astype(BF16), r_ref[...])


def _in_proj_kernel(x_ref, mod_ref, g_ref, w_ref, r_ref, gqk_ref, cos_ref, sin_ref,
                    dq_ref, dk_ref, dv_ref, lq_ref, lk_ref, lv_ref):
    d = x_ref.shape[-1]
    mod = mod_ref[0]
    h = _modulated_norm(x_ref[0], g_ref[...], mod[:, 0:d], mod[:, d:2 * d]).astype(BF16)
    dim = lax.broadcasted_iota(jnp.int32, cos_ref.shape[1:], 1) % HEAD_DIM
    first_half = dim < ROPE_HALF
    rope_c = jnp.where(dim < ROPE_DIM, cos_ref[0], 1.0)
    rope_s = jnp.where(first_half, -sin_ref[0], jnp.where(dim < ROPE_DIM, sin_ref[0], 0.0))
    width = dq_ref.shape[-1]
    for sec, o_ref in enumerate((dq_ref, dk_ref, dv_ref, lq_ref, lk_ref, lv_ref)):
        p = _dot(h, w_ref[:, sec * width:(sec + 1) * width])
        if o_ref is dv_ref or o_ref is lv_ref:
            o_ref[0] = p.astype(BF16)
            continue
        for c in range(width // MXU_DIM):
            col = sec * width + c * MXU_DIM
            xc = p[:, c * MXU_DIM:(c + 1) * MXU_DIM]
            y = xc * lax.rsqrt(_group_mean_sq(xc, r_ref) + EPS) * gqk_ref[:, col:col + MXU_DIM]
            for t in range(MXU_DIM // LANES):
                yt = y[:, t * LANES:(t + 1) * LANES]
                partner = jnp.where(first_half, pltpu.roll(yt, LANES - ROPE_HALF, 1), pltpu.roll(yt, ROPE_HALF, 1))
                out_col = c * MXU_DIM + t * LANES
                o_ref[0, :, out_col:out_col + LANES] = (yt * rope_c + partner * rope_s).astype(BF16)


def _in_proj_call(x, mod_l, g, w, rmat, gqk, cos_t, sin_t):
    b, s, d = x.shape
    tm = IN_PROJ_TILE
    tok = lambda w: pl.BlockSpec((1, tm, w), lambda i, j: (i, j, 0))
    width = w.shape[1] // 6
    out = jax.ShapeDtypeStruct((b, s, width), BF16)
    return pl.pallas_call(
        _in_proj_kernel,
        grid=(b, s // tm),
        in_specs=[
            tok(d),
            pl.BlockSpec((1, 1, mod_l.shape[-1]), lambda i, j: (i, 0, 0)),
            _const_spec((1, d)),
            _const_spec(w.shape, True), _const_spec(rmat.shape), _const_spec(gqk.shape),
            tok(LANES), tok(LANES),
        ],
        out_specs=[tok(width)] * 6,
        out_shape=[out] * 6,
        compiler_params=_params(("parallel", "parallel")),
        name="in_proj",
    )(x, mod_l, g, w, rmat, gqk, cos_t, sin_t)


def _diff_attn_kernel(bound_ref, q_ref, k_ref, v_ref, lam_ref, g_ref, o_ref):
    tq = DIFF_Q_BLOCK
    seq = k_ref.shape[1]
    nq = seq // tq
    heads = q_ref.shape[2] // LANES
    step_id = pl.program_id(2)
    lanes = lambda hh: slice(hh * LANES, (hh + 1) * LANES)

    def stacked_q(hh, q0):
        q = q_ref[0, pl.ds(q0, tq), lanes(hh)]
        lane = lax.broadcasted_iota(jnp.int32, q.shape, 1)
        zero = jnp.zeros_like(q)
        return jnp.concatenate([jnp.where(lane < HEAD_DIM, q, zero), jnp.where(lane >= HEAD_DIM, q, zero)], axis=0)

    def causal(s, k0):
        qry = lax.broadcasted_iota(jnp.int32, s.shape, 0) % tq
        key = lax.broadcasted_iota(jnp.int32, s.shape, 1) + k0
        return jnp.where(key <= qry, s, NEG_INF)

    def v_ext(hh, k0, n):
        return jnp.concatenate([v_ref[0, pl.ds(k0, n), lanes(hh)], jnp.ones((n, LANES), BF16)], axis=1)

    def finish(hh, q0, acc):
        o1 = acc[:tq, :LANES] / acc[:tq, LANES:]
        o2 = acc[tq:, :LANES] / acc[tq:, LANES:]
        o = o1 - lam_ref[0:1, 0:1] * o2
        ms = jnp.mean(o * o, axis=-1, keepdims=True)
        o_ref[0, pl.ds(q0, tq), lanes(hh)] = (o * lax.rsqrt(ms + EPS) * g_ref[...]).astype(o_ref.dtype)

    def unshifted(blk):
        kv = (blk + 1) * tq
        for hh in range(heads):
            s = causal(_dot_nt(stacked_q(hh, blk * tq), k_ref[0, :kv, lanes(hh)]), -blk * tq)
            finish(hh, blk * tq, _dot(jnp.exp2(s).astype(BF16), v_ext(hh, 0, kv)))

    def shifted(blk):
        q0 = pl.multiple_of(blk * tq, tq)
        for hh in range(heads):
            qb = stacked_q(hh, q0)

            def step(j, carry, diagonal, hh=hh, qb=qb):
                m, acc = carry
                k0 = pl.multiple_of(j * tq, tq)
                s = _dot_nt(qb, k_ref[0, pl.ds(k0, tq), lanes(hh)])
                if diagonal:
                    s = causal(s, 0)
                m_new = jnp.maximum(m, jnp.max(s, axis=-1, keepdims=True))
                acc = jnp.exp2(m - m_new) * acc + _dot(jnp.exp2(s - m_new).astype(BF16), v_ext(hh, k0, tq))
                return m_new, acc

            carry = (jnp.full((2 * tq, 1), NEG_INF, F32), jnp.zeros((2 * tq, 2 * LANES), F32))
            carry = lax.fori_loop(0, blk, lambda j, c, step=step: step(j, c, False), carry)
            finish(hh, q0, step(blk, carry, True)[1])

    safe = bound_ref[0] <= SAFE_LOGIT
    for pair in range(nq // 2):
        @pl.when(jnp.logical_and(step_id == pair, safe))
        def _(pair=pair):
            unshifted(pair)
            unshifted(nq - 1 - pair)

    @pl.when(jnp.logical_not(safe))
    def _():
        def one_block(which, _):
            shifted(jnp.where(which == 0, step_id, nq - 1 - step_id))
            return 0
        lax.fori_loop(0, 2, one_block, 0)


def _diff_attn_call(bound, dq, dk, dv, lam_l, g_sub):
    b, s, w = dq.shape
    cols = DIFF_HEADS_PER_STEP * LANES
    rows = pl.BlockSpec((1, s, cols), lambda i, h, j: (i, 0, h))
    return pl.pallas_call(
        _diff_attn_kernel,
        grid=(b, w // cols, s // DIFF_Q_BLOCK // 2),
        in_specs=[_SMEM_SPEC, rows, rows, rows, _const_spec((1, LANES)), _const_spec((1, LANES))],
        out_specs=rows,
        out_shape=jax.ShapeDtypeStruct((b, s, w), BF16),
        compiler_params=_params(("parallel", "parallel", "arbitrary")),
        name="diff_attn",
    )(bound, dq, dk, dv, lam_l, g_sub)


def _rows(start, size, stride):
    return pl.ds(start, size, stride=stride) if stride > 1 else pl.ds(start, size)


def _dil_attn_kernel(bound_ref, q_ref, k_ref, v_ref, r_ref, g_ref, o_ref,
                     stage, stage_mid, qp, kp, vm, num, den, mx, num_mid, den_mid):
    seq = q_ref.shape[1]
    blk = DIL_BLOCK
    head0 = lax.broadcasted_iota(jnp.int32, (blk, LANES), 1) < HEAD_DIM
    strides = tuple(stride for _, stride in DILATED_PAIRS)
    mid = strides[1]
    mid_len = seq // mid

    def mid_rows(r, n_rows):
        return pl.ds((r % mid) * mid_len + r // mid, n_rows, stride=mid)

    def stage_rows(di, dst_rows, q, k, v):
        lane0 = lax.broadcasted_iota(jnp.int32, v.shape, 1) < HEAD_DIM
        one = jnp.ones_like(v)
        if di > 0:
            qp[di - 1, dst_rows, :] = q
            kp[di - 1, dst_rows, :] = k
        vm[di, 0, dst_rows, :] = jnp.where(lane0, v, one)
        vm[di, 1, dst_rows, :] = jnp.where(lane0, one, v)

    stage_rows(0, slice(None), None, None, v_ref[0])
    for j, ref in enumerate((q_ref, k_ref, v_ref)):
        stage[j] = ref[0].astype(F32)
        for r in range(mid):
            stage_mid[j, r * mid_len:(r + 1) * mid_len, :] = stage[j, pl.ds(r, mid_len, stride=mid), :]
    stage_rows(1, slice(None), *(stage_mid[j].astype(BF16) for j in range(3)))
    last_len = seq // strides[2]
    for r in range(strides[2]):
        stage_rows(2, slice(r * last_len, (r + 1) * last_len),
                   *(stage_mid[j, mid_rows(r, last_len), :].astype(BF16) for j in range(3)))

    def q_rows(di, rows):
        return q_ref[0, rows, :] if di == 0 else qp[di - 1, rows, :]

    def k_rows(di, rows):
        return k_ref[0, rows, :] if di == 0 else kp[di - 1, rows, :]

    def window(ctx):
        qq = lax.broadcasted_iota(jnp.int32, (2 * blk, ctx), 0) % blk
        kk = lax.broadcasted_iota(jnp.int32, (2 * blk, ctx), 1)
        return (kk <= qq) if ctx == blk else ((kk >= qq) & (kk <= qq + blk))

    def block(di, r, n, mode):
        stride = strides[di]
        first = di == 0
        q0 = r * (seq // stride) + n * blk
        k0 = q0 - blk if n > 0 else q0
        if not isinstance(r, int):
            q0, k0 = pl.multiple_of(q0, blk), pl.multiple_of(k0, blk)
        ctx = 2 * blk if n > 0 else blk
        out_rows = _rows(r + n * blk * stride, blk, stride)
        q = q_rows(di, pl.ds(q0, blk))
        zero = jnp.zeros_like(q)
        qb = jnp.concatenate([jnp.where(head0, q, zero), jnp.where(head0, zero, q)], axis=0)
        s = _dot_nt(qb, k_rows(di, pl.ds(k0, ctx)))
        s = jnp.where(window(ctx), s, NEG_INF)
        if mode == "max":
            m = jnp.max(s, axis=-1, keepdims=True)
            m = jnp.where(head0, m[:blk], m[blk:])
            mx[out_rows, :] = m if first else jnp.maximum(mx[out_rows, :], m)
            return
        if mode == "shifted":
            m = mx[out_rows, :]
            s = s - jnp.concatenate([m[:, 0:1], m[:, HEAD_DIM:HEAD_DIM + 1]], axis=0)
        p = jnp.exp2(s).astype(BF16)
        out0 = _dot(p[:blk], vm[di, 0, pl.ds(k0, ctx), :])
        out1 = _dot(p[blk:], vm[di, 1, pl.ds(k0, ctx), :])
        n_blk = jnp.where(head0, out0, out1)
        d_blk = jnp.where(head0, out1, out0)
        if di == 0:
            num[out_rows, :] = n_blk
            den[out_rows, :] = d_blk
        elif di == 2:
            num_mid[mid_rows(r, blk), :] = n_blk
            den_mid[mid_rows(r, blk), :] = d_blk
        else:
            num[out_rows, :] = num[out_rows, :] + (n_blk + num_mid[pl.ds(q0, blk), :])
            den[out_rows, :] = den[out_rows, :] + (d_blk + den_mid[pl.ds(q0, blk), :])

    def sweep(mode, straight_line):
        for di in (0, 2, 1):
            stride = strides[di]
            nb = seq // stride // blk
            if straight_line or stride == 1:
                for r in range(stride):
                    for n in range(nb):
                        block(di, r, n, mode)
            else:
                def subsequence(r, _, di=di, nb=nb):
                    for n in range(nb):
                        block(di, r, n, mode)
                    return 0
                lax.fori_loop(0, stride, subsequence, 0)

    safe = bound_ref[0] <= SAFE_LOGIT

    @pl.when(safe)
    def _():
        sweep("plain", True)

    @pl.when(jnp.logical_not(safe))
    def _():
        sweep("max", False)
        sweep("shifted", False)

    o = num[...] / pltpu.roll(den[...], HEAD_DIM, 1)
    ms = _dot((o * o).astype(BF16), r_ref[...])
    o_ref[0] = (o * lax.rsqrt(ms + EPS) * g_ref[...]).astype(o_ref.dtype)


def _dil_attn_call(bound, lq, lk, lv, rmat, gd):
    b, s, w = lq.shape
    spec = pl.BlockSpec((1, s, LANES), lambda i, h: (i, 0, h))
    f32_rows = pltpu.VMEM((s, LANES), F32)
    npairs = len(DILATED_PAIRS)
    strides = tuple(stride for _, stride in DILATED_PAIRS)
    assert strides == (1, strides[1], strides[1] ** 2) and s // strides[2] == DIL_BLOCK
    assert all(window // stride == DIL_BLOCK for window, stride in DILATED_PAIRS)
    return pl.pallas_call(
        _dil_attn_kernel,
        grid=(b, w // LANES),
        in_specs=[_SMEM_SPEC, spec, spec, spec, _const_spec((LANES, LANES)),
                  pl.BlockSpec((1, LANES), lambda i, h: (0, h))],
        out_specs=spec,
        out_shape=jax.ShapeDtypeStruct((b, s, w), BF16),
        scratch_shapes=[pltpu.VMEM((3, s, LANES), F32),
                        pltpu.VMEM((3, s, LANES), F32),
                        pltpu.VMEM((npairs - 1, s, LANES), BF16),
                        pltpu.VMEM((npairs - 1, s, LANES), BF16),
                        pltpu.VMEM((npairs, 2, s, LANES), BF16),
                        f32_rows, f32_rows, f32_rows, f32_rows, f32_rows],
        compiler_params=_params(("parallel", "parallel")),
        name="dil_attn",
    )(bound, lq, lk, lv, rmat, gd)


def _mix_ffn_kernel(x_ref, mod_ref, oa_ref, ob_ref, wo_ref, g_ref, wu_ref, cw_ref, cb_ref, wd_ref,
                    xo_ref, tail_ref):
    d = x_ref.shape[-1]
    tm = x_ref.shape[1]
    d_ff = wd_ref.shape[0]
    nchunk = d_ff // FF_CHUNK
    halo = tail_ref.shape[2]
    wa = oa_ref.shape[-1]

    @pl.when(pl.program_id(1) == 0)
    def _():
        tail_ref[...] = jnp.zeros_like(tail_ref)

    mod = mod_ref[0]
    mixed = _dot(oa_ref[0], wo_ref[:wa, :]) + _dot(ob_ref[0], wo_ref[wa:, :])
    x = x_ref[0] + mod[:, 2 * d:3 * d] * mixed
    h = _modulated_norm(x, g_ref[...], mod[:, 3 * d:4 * d], mod[:, 4 * d:5 * d]).astype(BF16)
    row = lax.broadcasted_iota(jnp.int32, (halo, FF_CHUNK), 0)

    def chunk(c):
        convs = []
        for part in range(2):
            cols = slice(part * d_ff + c * FF_CHUNK, part * d_ff + (c + 1) * FF_CHUNK)
            u = _dot(h, wu_ref[:, cols])
            prev = tail_ref[c, part]
            tail_ref[c, part] = u[tm - halo:, :]
            conv = cb_ref[:, cols] + u * cw_ref[CONV_WIDTH - 1:CONV_WIDTH, cols]
            for lag in range(1, CONV_WIDTH):
                shifted = pltpu.roll(u, lag, 0)
                head = jnp.where(row < lag, pltpu.roll(prev, lag, 0), shifted[:halo])
                shifted = jnp.concatenate([head, shifted[halo:]], axis=0)
                conv = conv + shifted * cw_ref[CONV_WIDTH - 1 - lag:CONV_WIDTH - lag, cols]
            convs.append(conv)
        half_gate, val = convs
        return ((half_gate + half_gate * jnp.tanh(half_gate)) * val).astype(BF16)

    y = None
    for c0 in range(0, nchunk, FF_DOWN_GROUP):
        c1 = min(c0 + FF_DOWN_GROUP, nchunk)
        act = jnp.concatenate([chunk(c) for c in range(c0, c1)], axis=1)
        part = _dot(act, wd_ref[c0 * FF_CHUNK:c1 * FF_CHUNK, :])
        y = part if y is None else y + part
    xo_ref[0] = x + mod[:, 5 * d:6 * d] * y


def _mix_ffn_call(x, mod_l, oa, ob, wo, g, wu, cw, cb, wd):
    b, s, d = x.shape
    tm = FFN_TILE
    tok = lambda w: pl.BlockSpec((1, tm, w), lambda i, j: (i, j, 0))
    nchunk = wd.shape[0] // FF_CHUNK
    return pl.pallas_call(
        _mix_ffn_kernel,
        grid=(b, s // tm),
        in_specs=[tok(d), pl.BlockSpec((1, 1, mod_l.shape[-1]), lambda i, j: (i, 0, 0)),
                  tok(oa.shape[-1]), tok(ob.shape[-1]), _const_spec(wo.shape, True), _const_spec((1, d)),
                  _const_spec(wu.shape, True), _const_spec(cw.shape), _const_spec(cb.shape),
                  _const_spec(wd.shape, True)],
        out_specs=tok(d),
        out_shape=jax.ShapeDtypeStruct(x.shape, x.dtype),
        scratch_shapes=[pltpu.VMEM((nchunk, 2, SUBLANES, FF_CHUNK), F32)],
        compiler_params=_params(("parallel", "arbitrary")),
        name="mix_ffn",
    )(x, mod_l, oa, ob, wo, g, wu, cw, cb, wd)


def _group_mean_matrix(n):
    idx = np.arange(n) // HEAD_DIM
    return jnp.asarray((idx[:, None] == idx[None, :]).astype(np.float32) / HEAD_DIM, dtype=BF16)


def _logit_bound(g_q, g_k):
    return (HEAD_DIM * QK_SCALE * jnp.max(jnp.abs(g_q)) * jnp.max(jnp.abs(g_k))).reshape(1).astype(F32)


def kernel(x, c, positions, g_mix, g_ffn, w_ada, b_ada, w_in, w_out, diff_q_g, diff_k_g, lam_q1, lam_k1, lam_q2, lam_k2, diff_subln_g, dil_q_g, dil_k_g, dil_out_g, w_up, conv_w, conv_b, w_down):
    depth, d, _ = w_in.shape
    b = x.shape[0]
    d_ff = w_down.shape[1]

    lam_init = np.array([0.8 - 0.6 * math.exp(-0.3 * l) for l in range(depth)], np.float32)
    lam_init_tile = jnp.asarray(np.broadcast_to(lam_init[:, None, None], (depth, 1, LANES)))
    mod, lam = _mod_call(c, w_ada, b_ada, lam_q1, lam_k1, lam_q2, lam_k2, lam_init_tile)
    cos_t, sin_t = _rope_call(positions)
    rmat_qk = _group_mean_matrix(MXU_DIM)
    rmat_out = _group_mean_matrix(LANES)
    gate_half = jnp.concatenate([jnp.full((1, d_ff), 0.5, F32), jnp.ones((1, d_ff), F32)], axis=1)
    reps = DIFF_WIDTH // HEAD_DIM
    ones = jnp.ones((DIFF_WIDTH,), F32)

    for l in range(depth):
        gqk = jnp.concatenate([jnp.tile(diff_q_g[l], reps) * QK_SCALE, jnp.tile(diff_k_g[l], reps), ones,
                               jnp.tile(dil_q_g[l], reps) * QK_SCALE, jnp.tile(dil_k_g[l], reps), ones]).reshape(1, -1)
        mod_l = mod[l].reshape(b, 1, -1)
        dq, dk, dv, lq, lk, lv = _in_proj_call(x, mod_l, g_mix[l].reshape(1, d), w_in[l].astype(BF16), rmat_qk, gqk,
                                               cos_t, sin_t)
        g_sub = (diff_subln_g[l] * (1.0 - float(lam_init[l]))).reshape(1, DIFF_V_DIM)
        oa = _diff_attn_call(_logit_bound(diff_q_g[l], diff_k_g[l]), dq, dk, dv, lam[l], g_sub)
        gd = jnp.tile(dil_out_g[l], N_DIL_HEADS).reshape(1, DIL_WIDTH)
        ob = _dil_attn_call(_logit_bound(dil_q_g[l], dil_k_g[l]), lq, lk, lv, rmat_out, gd)
        x = _mix_ffn_call(x, mod_l, oa, ob, w_out[l].astype(BF16), g_ffn[l].reshape(1, d), w_up[l].astype(BF16),
                          conv_w[l] * gate_half, conv_b[l].reshape(1, -1) * gate_half, w_down[l].astype(BF16))
    return x
```

```python
import math

import jax
import jax.numpy as jnp
import numpy as np
from jax import lax
from jax.experimental import pallas as pl
from jax.experimental.pallas import tpu as pltpu

HEAD_DIM = 64
N_DIFF_HEADS = 4
DIFF_V_DIM = 2 * HEAD_DIM
DIFF_WIDTH = N_DIFF_HEADS * DIFF_V_DIM
N_DIL_HEADS = 8
DIL_WIDTH = N_DIL_HEADS * HEAD_DIM
DILATED_PAIRS = ((128, 1), (512, 4), (2048, 16))
ROPE_THETA = 500000.0
ROPE_DIM = HEAD_DIM // 4
ROPE_HALF = ROPE_DIM // 2
CONV_WIDTH = 3
EPS = 1e-6
NEG_INF = -1e30
LOG2E = 1.4426950408889634
QK_SCALE = HEAD_DIM ** -0.5 * LOG2E

LANES = 128
SUBLANES = 8
MXU_DIM = 256
DIFF_Q_BLOCK = 256
FFN_TILE = 512
IN_PROJ_TILE = 1024
DIFF_HEADS_PER_STEP = 4
DIL_BLOCK = 128
DIL_LANE_BLOCKS_PER_STEP = 2
FF_CHUNK = 256
FF_DOWN_GROUP = 6
SAFE_LOGIT = 64.0
VMEM_LIMIT = 56 * 1024 * 1024

F32 = jnp.float32
BF16 = jnp.bfloat16


def _dot(a, b):
    return jnp.dot(a, b, preferred_element_type=F32)


def _dot_nt(a, b):
    return lax.dot_general(a, b, (((1,), (1,)), ((), ())), preferred_element_type=F32)


def _split_bf16(x):
    hi = x.astype(BF16)
    lo = (x - hi.astype(F32)).astype(BF16)
    return hi, lo


def _params(sem, vmem=VMEM_LIMIT):
    return pltpu.CompilerParams(dimension_semantics=sem, vmem_limit_bytes=vmem)


def _const_spec(shape, single_buffer=False):
    mode = pl.Buffered(1) if single_buffer else None
    return pl.BlockSpec(shape, lambda *_: (0,) * len(shape), pipeline_mode=mode)


_SMEM_SPEC = pl.BlockSpec(memory_space=pltpu.SMEM)


def _mod_kernel(c_ref, w_ref, b_ref, q1_ref, k1_ref, q2_ref, k2_ref, li_ref, mod_ref, lam_ref):
    c = c_ref[...]
    cond = c * (1.0 / (1.0 + jnp.exp(-c)))
    ch, cl = _split_bf16(cond)
    wh, wl = _split_bf16(w_ref[0])
    mod_ref[0] = _dot(ch, wh) + (_dot(ch, wl) + _dot(cl, wh)) + b_ref[0]
    s1 = jnp.sum(q1_ref[0] * k1_ref[0], axis=-1, keepdims=True)
    s2 = jnp.sum(q2_ref[0] * k2_ref[0], axis=-1, keepdims=True)
    lam_ref[0] = (jnp.exp(s1) - jnp.exp(s2)) + li_ref[0]


def _mod_call(c, w_ada, b_ada, lam_q1, lam_k1, lam_q2, lam_k2, lam_init):
    depth, d, d6 = w_ada.shape
    b = c.shape[0]
    nj = d6 // d
    vec = lambda a: a.reshape(depth, 1, HEAD_DIM)
    vspec = pl.BlockSpec((1, 1, HEAD_DIM), lambda l, j: (l, 0, 0))
    return pl.pallas_call(
        _mod_kernel,
        grid=(depth, nj),
        in_specs=[
            pl.BlockSpec((b, d), lambda l, j: (0, 0)),
            pl.BlockSpec((1, d, d), lambda l, j: (l, 0, j)),
            pl.BlockSpec((1, 1, d), lambda l, j: (l, 0, j)),
            vspec, vspec, vspec, vspec,
            pl.BlockSpec((1, 1, LANES), lambda l, j: (l, 0, 0)),
        ],
        out_specs=[
            pl.BlockSpec((1, b, d), lambda l, j: (l, 0, j)),
            pl.BlockSpec((1, 1, LANES), lambda l, j: (l, 0, 0)),
        ],
        out_shape=[
            jax.ShapeDtypeStruct((depth, b, d6), F32),
            jax.ShapeDtypeStruct((depth, 1, LANES), F32),
        ],
        compiler_params=_params(("arbitrary", "arbitrary")),
        name="adaln_mod",
    )(c, w_ada, b_ada.reshape(depth, 1, d6), vec(lam_q1), vec(lam_k1), vec(lam_q2), vec(lam_k2), lam_init)


def _rope_kernel(pos_ref, invf_ref, c_ref, s_ref):
    ang = pos_ref[0].astype(F32) * invf_ref[...]
    c_ref[0] = jnp.cos(ang)
    s_ref[0] = jnp.sin(ang)


def _rope_call(positions):
    b, s = positions.shape
    per_row = LANES // ROPE_HALF
    rows = s // per_row
    inv_freq = ROPE_THETA ** (-jnp.arange(0, ROPE_DIM, 2, dtype=F32) / ROPE_DIM)
    invf = jnp.tile(inv_freq, per_row).reshape(1, LANES)
    pos = jnp.repeat(positions.reshape(b, rows, per_row), ROPE_HALF, axis=-1)
    spec = pl.BlockSpec((1, rows, LANES), lambda i: (i, 0, 0))
    out = jax.ShapeDtypeStruct((b, rows, LANES), F32)
    cos, sin = pl.pallas_call(
        _rope_kernel,
        grid=(b,),
        in_specs=[spec, _const_spec((1, LANES))],
        out_specs=[spec, spec],
        out_shape=[out, out],
        compiler_params=_params(("parallel",)),
        name="rope_tables",
    )(pos, invf)
    lane_tile = lambda t: jnp.tile(t.reshape(b, s, ROPE_HALF), (1, 1, per_row))
    return lane_tile(cos), lane_tile(sin)


def _modulated_norm(x, g, shift, scale):
    ms = jnp.mean(x * x, axis=-1, keepdims=True)
    return (x * lax.rsqrt(ms + EPS) * g) * (1.0 + scale) + shift


def _group_mean_sq(y, r_ref):
    return _dot((y * y).astype(BF16), r_ref[...])


def _in_proj_kernel(x_ref, mod_ref, g_ref, w_ref, r_ref, gqk_ref, cos_ref, sin_ref,
                    dq_ref, dk_ref, dv_ref, lq_ref, lk_ref, lv_ref):
    d = x_ref.shape[-1]
    mod = mod_ref[0]
    h = _modulated_norm(x_ref[0], g_ref[...], mod[:, 0:d], mod[:, d:2 * d]).astype(BF16)
    dim = lax.broadcasted_iota(jnp.int32, cos_ref.shape[1:], 1) % HEAD_DIM
    first_half = dim < ROPE_HALF
    rope_c = jnp.where(dim < ROPE_DIM, cos_ref[0], 1.0)
    rope_s = jnp.where(first_half, -sin_ref[0], jnp.where(dim < ROPE_DIM, sin_ref[0], 0.0))
    width = dq_ref.shape[-1]
    for sec, o_ref in enumerate((dq_ref, dk_ref, dv_ref, lq_ref, lk_ref, lv_ref)):
        p = _dot(h, w_ref[:, sec * width:(sec + 1) * width])
        if o_ref is dv_ref or o_ref is lv_ref:
            o_ref[0] = p.astype(BF16)
            continue
        for c in range(width // MXU_DIM):
            col = sec * width + c * MXU_DIM
            xc = p[:, c * MXU_DIM:(c + 1) * MXU_DIM]
            y = xc * lax.rsqrt(_group_mean_sq(xc, r_ref) + EPS) * gqk_ref[:, col:col + MXU_DIM]
            for t in range(MXU_DIM // LANES):
                yt = y[:, t * LANES:(t + 1) * LANES]
                partner = jnp.where(first_half, pltpu.roll(yt, LANES - ROPE_HALF, 1), pltpu.roll(yt, ROPE_HALF, 1))
                out_col = c * MXU_DIM + t * LANES
                o_ref[0, :, out_col:out_col + LANES] = (yt * rope_c + partner * rope_s).astype(BF16)


def _in_proj_call(x, mod_l, g, w, rmat, gqk, cos_t, sin_t):
    b, s, d = x.shape
    tm = IN_PROJ_TILE
    tok = lambda w: pl.BlockSpec((1, tm, w), lambda i, j: (i, j, 0))
    width = w.shape[1] // 6
    out = jax.ShapeDtypeStruct((b, s, width), BF16)
    return pl.pallas_call(
        _in_proj_kernel,
        grid=(b, s // tm),
        in_specs=[
            tok(d),
            pl.BlockSpec((1, 1, mod_l.shape[-1]), lambda i, j: (i, 0, 0)),
            _const_spec((1, d)),
            _const_spec(w.shape, True), _const_spec(rmat.shape), _const_spec(gqk.shape),
            tok(LANES), tok(LANES),
        ],
        out_specs=[tok(width)] * 6,
        out_shape=[out] * 6,
        compiler_params=_params(("parallel", "parallel")),
        name="in_proj",
    )(x, mod_l, g, w, rmat, gqk, cos_t, sin_t)


def _diff_attn_kernel(bound_ref, q_ref, k_ref, v_ref, lam_ref, g_ref, o_ref):
    tq = DIFF_Q_BLOCK
    seq = k_ref.shape[1]
    nq = seq // tq
    heads = q_ref.shape[2] // LANES
    step_id = pl.program_id(2)
    lanes = lambda hh: slice(hh * LANES, (hh + 1) * LANES)

    def stacked_q(hh, q0):
        q = q_ref[0, pl.ds(q0, tq), lanes(hh)]
        lane = lax.broadcasted_iota(jnp.int32, q.shape, 1)
        zero = jnp.zeros_like(q)
        return jnp.concatenate([jnp.where(lane < HEAD_DIM, q, zero), jnp.where(lane >= HEAD_DIM, q, zero)], axis=0)

    def causal(s, k0):
        qry = lax.broadcasted_iota(jnp.int32, s.shape, 0) % tq
        key = lax.broadcasted_iota(jnp.int32, s.shape, 1) + k0
        return jnp.where(key <= qry, s, NEG_INF)

    def v_ext(hh, k0, n):
        return jnp.concatenate([v_ref[0, pl.ds(k0, n), lanes(hh)], jnp.ones((n, LANES), BF16)], axis=1)

    def finish(hh, q0, acc):
        o1 = acc[:tq, :LANES] / acc[:tq, LANES:]
        o2 = acc[tq:, :LANES] / acc[tq:, LANES:]
        o = o1 - lam_ref[0:1, 0:1] * o2
        ms = jnp.mean(o * o, axis=-1, keepdims=True)
        o_ref[0, pl.ds(q0, tq), lanes(hh)] = (o * lax.rsqrt(ms + EPS) * g_ref[...]).astype(o_ref.dtype)

    def unshifted(blk):
        kv = (blk + 1) * tq
        for hh in range(heads):
            s = causal(_dot_nt(stacked_q(hh, blk * tq), k_ref[0, :kv, lanes(hh)]), -blk * tq)
            finish(hh, blk * tq, _dot(jnp.exp2(s).astype(BF16), v_ext(hh, 0, kv)))

    def shifted(blk):
        q0 = pl.multiple_of(blk * tq, tq)
        for hh in range(heads):
            qb = stacked_q(hh, q0)

            def step(j, carry, diagonal, hh=hh, qb=qb):
                m, acc = carry
                k0 = pl.multiple_of(j * tq, tq)
                s = _dot_nt(qb, k_ref[0, pl.ds(k0, tq), lanes(hh)])
                if diagonal:
                    s = causal(s, 0)
                m_new = jnp.maximum(m, jnp.max(s, axis=-1, keepdims=True))
                acc = jnp.exp2(m - m_new) * acc + _dot(jnp.exp2(s - m_new).astype(BF16), v_ext(hh, k0, tq))
                return m_new, acc

            carry = (jnp.full((2 * tq, 1), NEG_INF, F32), jnp.zeros((2 * tq, 2 * LANES), F32))
            carry = lax.fori_loop(0, blk, lambda j, c, step=step: step(j, c, False), carry)
            finish(hh, q0, step(blk, carry, True)[1])

    safe = bound_ref[0] <= SAFE_LOGIT
    for pair in range(nq // 2):
        @pl.when(jnp.logical_and(step_id == pair, safe))
        def _(pair=pair):
            unshifted(pair)
            unshifted(nq - 1 - pair)

    @pl.when(jnp.logical_not(safe))
    def _():
        def one_block(which, _):
            shifted(jnp.where(which == 0, step_id, nq - 1 - step_id))
            return 0
        lax.fori_loop(0, 2, one_block, 0)


def _diff_attn_call(bound, dq, dk, dv, lam_l, g_sub):
    b, s, w = dq.shape
    cols = DIFF_HEADS_PER_STEP * LANES
    rows = pl.BlockSpec((1, s, cols), lambda i, h, j: (i, 0, h))
    return pl.pallas_call(
        _diff_attn_kernel,
        grid=(b, w // cols, s // DIFF_Q_BLOCK // 2),
        in_specs=[_SMEM_SPEC, rows, rows, rows, _const_spec((1, LANES)), _const_spec((1, LANES))],
        out_specs=rows,
        out_shape=jax.ShapeDtypeStruct((b, s, w), BF16),
        compiler_params=_params(("parallel", "parallel", "arbitrary")),
        name="diff_attn",
    )(bound, dq, dk, dv, lam_l, g_sub)


def _rows(start, size, stride):
    return pl.ds(start, size, stride=stride) if stride > 1 else pl.ds(start, size)


def _dil_lane_block(q_ref, k_ref, v_ref, r_ref, g_ref, o_ref,
                    stage, stage_mid, qp, kp, vm, num, den, mx, num_mid, den_mid):
    seq = q_ref.shape[1]
    blk = DIL_BLOCK
    head0 = lax.broadcasted_iota(jnp.int32, (blk, LANES), 1) < HEAD_DIM
    strides = tuple(stride for _, stride in DILATED_PAIRS)
    mid = strides[1]
    mid_len = seq // mid

    def mid_rows(r, n_rows):
        return pl.ds((r % mid) * mid_len + r // mid, n_rows, stride=mid)

    def stage_rows(di, dst_rows, q, k, v):
        lane0 = lax.broadcasted_iota(jnp.int32, v.shape, 1) < HEAD_DIM
        one = jnp.ones_like(v)
        if di > 0:
            qp[di - 1, dst_rows, :] = q
            kp[di - 1, dst_rows, :] = k
        vm[di, 0, dst_rows, :] = jnp.where(lane0, v, one)
        vm[di, 1, dst_rows, :] = jnp.where(lane0, one, v)

    def stage_all():
        stage_rows(0, slice(None), None, None, v_ref[0])
        for j, ref in enumerate((q_ref, k_ref, v_ref)):
            stage[j] = ref[0].astype(F32)
            for r in range(mid):
                stage_mid[j, r * mid_len:(r + 1) * mid_len, :] = stage[j, pl.ds(r, mid_len, stride=mid), :]
        stage_rows(1, slice(None), *(stage_mid[j].astype(BF16) for j in range(3)))
        last_len = seq // strides[2]
        for r in range(strides[2]):
            stage_rows(2, slice(r * last_len, (r + 1) * last_len),
                       *(stage_mid[j, mid_rows(r, last_len), :].astype(BF16) for j in range(3)))

    def q_rows(di, rows):
        return q_ref[0, rows, :] if di == 0 else qp[di - 1, rows, :]

    def k_rows(di, rows):
        return k_ref[0, rows, :] if di == 0 else kp[di - 1, rows, :]

    def window(ctx):
        qq = lax.broadcasted_iota(jnp.int32, (2 * blk, ctx), 0) % blk
        kk = lax.broadcasted_iota(jnp.int32, (2 * blk, ctx), 1)
        return (kk <= qq) if ctx == blk else ((kk >= qq) & (kk <= qq + blk))

    def block(di, r, n, mode):
        stride = strides[di]
        first = di == 0
        q0 = r * (seq // stride) + n * blk
        k0 = q0 - blk if n > 0 else q0
        if not isinstance(r, int):
            q0, k0 = pl.multiple_of(q0, blk), pl.multiple_of(k0, blk)
        ctx = 2 * blk if n > 0 else blk
        out_rows = _rows(r + n * blk * stride, blk, stride)
        q = q_rows(di, pl.ds(q0, blk))
        zero = jnp.zeros_like(q)
        qb = jnp.concatenate([jnp.where(head0, q, zero), jnp.where(head0, zero, q)], axis=0)
        s = _dot_nt(qb, k_rows(di, pl.ds(k0, ctx)))
        s = jnp.where(window(ctx), s, NEG_INF)
        if mode == "max":
            m = jnp.max(s, axis=-1, keepdims=True)
            m = jnp.where(head0, m[:blk], m[blk:])
            mx[out_rows, :] = m if first else jnp.maximum(mx[out_rows, :], m)
            return
        if mode == "shifted":
            m = mx[out_rows, :]
            s = s - jnp.concatenate([m[:, 0:1], m[:, HEAD_DIM:HEAD_DIM + 1]], axis=0)
        p = jnp.exp2(s).astype(BF16)
        out0 = _dot(p[:blk], vm[di, 0, pl.ds(k0, ctx), :])
        out1 = _dot(p[blk:], vm[di, 1, pl.ds(k0, ctx), :])
        n_blk = jnp.where(head0, out0, out1)
        d_blk = jnp.where(head0, out1, out0)
        if di == 0:
            num[out_rows, :] = n_blk
            den[out_rows, :] = d_blk
        elif di == 2:
            num_mid[mid_rows(r, blk), :] = n_blk
            den_mid[mid_rows(r, blk), :] = d_blk
        else:
            num[out_rows, :] = num[out_rows, :] + (n_blk + num_mid[pl.ds(q0, blk), :])
            den[out_rows, :] = den[out_rows, :] + (d_blk + den_mid[pl.ds(q0, blk), :])

    def sweep(mode, straight_line):
        for di in (0, 2, 1):
            stride = strides[di]
            nb = seq // stride // blk
            if straight_line or stride == 1:
                for r in range(stride):
                    for n in range(nb):
                        block(di, r, n, mode)
            else:
                def subsequence(r, _, di=di, nb=nb):
                    for n in range(nb):
                        block(di, r, n, mode)
                    return 0
                lax.fori_loop(0, stride, subsequence, 0)

    def finish():
        o = num[...] / pltpu.roll(den[...], HEAD_DIM, 1)
        ms = _dot((o * o).astype(BF16), r_ref[...])
        o_ref[0] = (o * lax.rsqrt(ms + EPS) * g_ref[...]).astype(o_ref.dtype)

    return stage_all, sweep, finish


def _dil_attn_kernel(bound_ref, q_ref, k_ref, v_ref, r_ref, g_ref, o_ref, *scratch):
    parts = []
    for hp in range(q_ref.shape[2] // LANES):
        cols = pl.ds(hp * LANES, LANES)
        parts.append(_dil_lane_block(q_ref.at[:, :, cols], k_ref.at[:, :, cols], v_ref.at[:, :, cols], r_ref,
                                     g_ref.at[:, cols], o_ref.at[:, :, cols], *(ref.at[hp] for ref in scratch)))
    safe = bound_ref[0] <= SAFE_LOGIT

    @pl.when(safe)
    def _():
        for stage_all, _, _ in parts:
            stage_all()
        for _, sweep, _ in parts:
            sweep("plain", True)
        for _, _, finish in parts:
            finish()

    @pl.when(jnp.logical_not(safe))
    def _():
        for stage_all, sweep, finish in parts:
            stage_all()
            sweep("max", False)
            sweep("shifted", False)
            finish()


def _dil_attn_call(bound, lq, lk, lv, rmat, gd):
    b, s, w = lq.shape
    per_step = DIL_LANE_BLOCKS_PER_STEP
    cols = per_step * LANES
    spec = pl.BlockSpec((1, s, cols), lambda i, h: (i, 0, h))
    f32_rows = pltpu.VMEM((per_step, s, LANES), F32)
    npairs = len(DILATED_PAIRS)
    strides = tuple(stride for _, stride in DILATED_PAIRS)
    assert strides == (1, strides[1], strides[1] ** 2) and s // strides[2] == DIL_BLOCK
    assert all(window // stride == DIL_BLOCK for window, stride in DILATED_PAIRS)
    return pl.pallas_call(
        _dil_attn_kernel,
        grid=(b, w // cols),
        in_specs=[_SMEM_SPEC, spec, spec, spec, _const_spec((LANES, LANES)),
                  pl.BlockSpec((1, cols), lambda i, h: (0, h))],
        out_specs=spec,
        out_shape=jax.ShapeDtypeStruct((b, s, w), BF16),
        scratch_shapes=[pltpu.VMEM((per_step, 3, s, LANES), F32),
                        pltpu.VMEM((per_step, 3, s, LANES), F32),
                        pltpu.VMEM((per_step, npairs - 1, s, LANES), BF16),
                        pltpu.VMEM((per_step, npairs - 1, s, LANES), BF16),
                        pltpu.VMEM((per_step, npairs, 2, s, LANES), BF16),
                        f32_rows, f32_rows, f32_rows, f32_rows, f32_rows],
        compiler_params=_params(("parallel", "parallel")),
        name="dil_attn",
    )(bound, lq, lk, lv, rmat, gd)


def _mix_ffn_kernel(x_ref, mod_ref, oa_ref, ob_ref, wo_ref, g_ref, wu_ref, cw_ref, cb_ref, wd_ref,
                    xo_ref, tail_ref):
    d = x_ref.shape[-1]
    tm = x_ref.shape[1]
    d_ff = wd_ref.shape[0]
    nchunk = d_ff // FF_CHUNK
    halo = tail_ref.shape[2]
    wa = oa_ref.shape[-1]

    @pl.when(pl.program_id(1) == 0)
    def _():
        tail_ref[...] = jnp.zeros_like(tail_ref)

    mod = mod_ref[0]
    mixed = _dot(oa_ref[0], wo_ref[:wa, :]) + _dot(ob_ref[0], wo_ref[wa:, :])
    x = x_ref[0] + mod[:, 2 * d:3 * d] * mixed
    h = _modulated_norm(x, g_ref[...], mod[:, 3 * d:4 * d], mod[:, 4 * d:5 * d]).astype(BF16)
    row = lax.broadcasted_iota(jnp.int32, (halo, FF_CHUNK), 0)

    def chunk(c):
        convs = []
        for part in range(2):
            cols = slice(part * d_ff + c * FF_CHUNK, part * d_ff + (c + 1) * FF_CHUNK)
            u = _dot(h, wu_ref[:, cols])
            prev = tail_ref[c, part]
            tail_ref[c, part] = u[tm - halo:, :]
            conv = cb_ref[:, cols] + u * cw_ref[CONV_WIDTH - 1:CONV_WIDTH, cols]
            for lag in range(1, CONV_WIDTH):
                shifted = pltpu.roll(u, lag, 0)
                head = jnp.where(row < lag, pltpu.roll(prev, lag, 0), shifted[:halo])
                shifted = jnp.concatenate([head, shifted[halo:]], axis=0)
                conv = conv + shifted * cw_ref[CONV_WIDTH - 1 - lag:CONV_WIDTH - lag, cols]
            convs.append(conv)
        half_gate, val = convs
        return ((half_gate + half_gate * jnp.tanh(half_gate)) * val).astype(BF16)

    y = None
    for c0 in range(0, nchunk, FF_DOWN_GROUP):
        c1 = min(c0 + FF_DOWN_GROUP, nchunk)
        act = jnp.concatenate([chunk(c) for c in range(c0, c1)], axis=1)
        part = _dot(act, wd_ref[c0 * FF_CHUNK:c1 * FF_CHUNK, :])
        y = part if y is None else y + part
    xo_ref[0] = x + mod[:, 5 * d:6 * d] * y


def _mix_ffn_call(x, mod_l, oa, ob, wo, g, wu, cw, cb, wd):
    b, s, d = x.shape
    tm = FFN_TILE
    tok = lambda w: pl.BlockSpec((1, tm, w), lambda i, j: (i, j, 0))
    nchunk = wd.shape[0] // FF_CHUNK
    return pl.pallas_call(
        _mix_ffn_kernel,
        grid=(b, s // tm),
        in_specs=[tok(d), pl.BlockSpec((1, 1, mod_l.shape[-1]), lambda i, j: (i, 0, 0)),
                  tok(oa.shape[-1]), tok(ob.shape[-1]), _const_spec(wo.shape, True), _const_spec((1, d)),
                  _const_spec(wu.shape, True), _const_spec(cw.shape), _const_spec(cb.shape),
                  _const_spec(wd.shape, True)],
        out_specs=tok(d),
        out_shape=jax.ShapeDtypeStruct(x.shape, x.dtype),
        scratch_shapes=[pltpu.VMEM((nchunk, 2, SUBLANES, FF_CHUNK), F32)],
        compiler_params=_params(("parallel", "arbitrary")),
        name="mix_ffn",
    )(x, mod_l, oa, ob, wo, g, wu, cw, cb, wd)


def _group_mean_matrix(n):
    idx = np.arange(n) // HEAD_DIM
    return jnp.asarray((idx[:, None] == idx[None, :]).astype(np.float32) / HEAD_DIM, dtype=BF16)


def _logit_bound(g_q, g_k):
    return (HEAD_DIM * QK_SCALE * jnp.max(jnp.abs(g_q)) * jnp.max(jnp.abs(g_k))).reshape(1).astype(F32)


def kernel(x, c, positions, g_mix, g_ffn, w_ada, b_ada, w_in, w_out, diff_q_g, diff_k_g, lam_q1, lam_k1, lam_q2, lam_k2, diff_subln_g, dil_q_g, dil_k_g, dil_out_g, w_up, conv_w, conv_b, w_down):
    depth, d, _ = w_in.shape
    b = x.shape[0]
    d_ff = w_down.shape[1]

    lam_init = np.array([0.8 - 0.6 * math.exp(-0.3 * l) for l in range(depth)], np.float32)
    lam_init_tile = jnp.asarray(np.broadcast_to(lam_init[:, None, None], (depth, 1, LANES)))
    mod, lam = _mod_call(c, w_ada, b_ada, lam_q1, lam_k1, lam_q2, lam_k2, lam_init_tile)
    cos_t, sin_t = _rope_call(positions)
    rmat_qk = _group_mean_matrix(MXU_DIM)
    rmat_out = _group_mean_matrix(LANES)
    gate_half = jnp.concatenate([jnp.full((1, d_ff), 0.5, F32), jnp.ones((1, d_ff), F32)], axis=1)
    reps = DIFF_WIDTH // HEAD_DIM
    ones = jnp.ones((DIFF_WIDTH,), F32)

    for l in range(depth):
        gqk = jnp.concatenate([jnp.tile(diff_q_g[l], reps) * QK_SCALE, jnp.tile(diff_k_g[l], reps), ones,
                               jnp.tile(dil_q_g[l], reps) * QK_SCALE, jnp.tile(dil_k_g[l], reps), ones]).reshape(1, -1)
        mod_l = mod[l].reshape(b, 1, -1)
        dq, dk, dv, lq, lk, lv = _in_proj_call(x, mod_l, g_mix[l].reshape(1, d), w_in[l].astype(BF16), rmat_qk, gqk,
                                               cos_t, sin_t)
        g_sub = (diff_subln_g[l] * (1.0 - float(lam_init[l]))).reshape(1, DIFF_V_DIM)
        oa = _diff_attn_call(_logit_bound(diff_q_g[l], diff_k_g[l]), dq, dk, dv, lam[l], g_sub)
        gd = jnp.tile(dil_out_g[l], N_DIL_HEADS).reshape(1, DIL_WIDTH)
        ob = _dil_attn_call(_logit_bound(dil_q_g[l], dil_k_g[l]), lq, lk, lv, rmat_out, gd)
        x = _mix_ffn_call(x, mod_l, oa, ob, w_out[l].astype(BF16), g_ffn[l].reshape(1, d), w_up[l].astype(BF16),
                          conv_w[l] * gate_half, conv_b[l].reshape(1, -1) * gate_half, w_down[l].astype(BF16))
    return x
```

```python
import math

import jax
import jax.numpy as jnp
import numpy as np
from jax import lax
from jax.experimental import pallas as pl
from jax.experimental.pallas import tpu as pltpu

HEAD_DIM = 64
N_DIFF_HEADS = 4
DIFF_V_DIM = 2 * HEAD_DIM
DIFF_WIDTH = N_DIFF_HEADS * DIFF_V_DIM
N_DIL_HEADS = 8
DIL_WIDTH = N_DIL_HEADS * HEAD_DIM
DILATED_PAIRS = ((128, 1), (512, 4), (2048, 16))
ROPE_THETA = 500000.0
ROPE_DIM = HEAD_DIM // 4
ROPE_HALF = ROPE_DIM // 2
CONV_WIDTH = 3
EPS = 1e-6
NEG_INF = -1e30
LOG2E = 1.4426950408889634
QK_SCALE = HEAD_DIM ** -0.5 * LOG2E

LANES = 128
SUBLANES = 8
MXU_DIM = 256
DIFF_Q_BLOCK = 256
FFN_TILE = 512
IN_PROJ_TILE = 1024
DIFF_HEADS_PER_STEP = 4
DIL_BLOCK = 128
DIL_LANE_BLOCKS_PER_STEP = 2
FF_CHUNK = 256
FF_DOWN_GROUP = 11
SAFE_LOGIT = 64.0
VMEM_LIMIT = 56 * 1024 * 1024

F32 = jnp.float32
BF16 = jnp.bfloat16


def _dot(a, b):
    return jnp.dot(a, b, preferred_element_type=F32)


def _dot_nt(a, b):
    return lax.dot_general(a, b, (((1,), (1,)), ((), ())), preferred_element_type=F32)


def _split_bf16(x):
    hi = x.astype(BF16)
    lo = (x - hi.astype(F32)).astype(BF16)
    return hi, lo


def _params(sem, vmem=VMEM_LIMIT):
    return pltpu.CompilerParams(dimension_semantics=sem, vmem_limit_bytes=vmem)


def _const_spec(shape, single_buffer=False):
    mode = pl.Buffered(1) if single_buffer else None
    return pl.BlockSpec(shape, lambda *_: (0,) * len(shape), pipeline_mode=mode)


_SMEM_SPEC = pl.BlockSpec(memory_space=pltpu.SMEM)


def _mod_kernel(c_ref, w_ref, b_ref, q1_ref, k1_ref, q2_ref, k2_ref, li_ref, mod_ref, lam_ref):
    c = c_ref[...]
    cond = c * (1.0 / (1.0 + jnp.exp(-c)))
    ch, cl = _split_bf16(cond)
    wh, wl = _split_bf16(w_ref[0])
    mod_ref[0] = _dot(ch, wh) + (_dot(ch, wl) + _dot(cl, wh)) + b_ref[0]
    s1 = jnp.sum(q1_ref[0] * k1_ref[0], axis=-1, keepdims=True)
    s2 = jnp.sum(q2_ref[0] * k2_ref[0], axis=-1, keepdims=True)
    lam_ref[0] = (jnp.exp(s1) - jnp.exp(s2)) + li_ref[0]


def _mod_call(c, w_ada, b_ada, lam_q1, lam_k1, lam_q2, lam_k2, lam_init):
    depth, d, d6 = w_ada.shape
    b = c.shape[0]
    nj = d6 // d
    vec = lambda a: a.reshape(depth, 1, HEAD_DIM)
    vspec = pl.BlockSpec((1, 1, HEAD_DIM), lambda l, j: (l, 0, 0))
    return pl.pallas_call(
        _mod_kernel,
        grid=(depth, nj),
        in_specs=[
            pl.BlockSpec((b, d), lambda l, j: (0, 0)),
            pl.BlockSpec((1, d, d), lambda l, j: (l, 0, j)),
            pl.BlockSpec((1, 1, d), lambda l, j: (l, 0, j)),
            vspec, vspec, vspec, vspec,
            pl.BlockSpec((1, 1, LANES), lambda l, j: (l, 0, 0)),
        ],
        out_specs=[
            pl.BlockSpec((1, b, d), lambda l, j: (l, 0, j)),
            pl.BlockSpec((1, 1, LANES), lambda l, j: (l, 0, 0)),
        ],
        out_shape=[
            jax.ShapeDtypeStruct((depth, b, d6), F32),
            jax.ShapeDtypeStruct((depth, 1, LANES), F32),
        ],
        compiler_params=_params(("arbitrary", "arbitrary")),
        name="adaln_mod",
    )(c, w_ada, b_ada.reshape(depth, 1, d6), vec(lam_q1), vec(lam_k1), vec(lam_q2), vec(lam_k2), lam_init)


def _rope_kernel(pos_ref, invf_ref, c_ref, s_ref):
    ang = pos_ref[0].astype(F32) * invf_ref[...]
    c_ref[0] = jnp.cos(ang)
    s_ref[0] = jnp.sin(ang)


def _rope_call(positions):
    b, s = positions.shape
    per_row = LANES // ROPE_HALF
    rows = s // per_row
    inv_freq = ROPE_THETA ** (-jnp.arange(0, ROPE_DIM, 2, dtype=F32) / ROPE_DIM)
    invf = jnp.tile(inv_freq, per_row).reshape(1, LANES)
    pos = jnp.repeat(positions.reshape(b, rows, per_row), ROPE_HALF, axis=-1)
    spec = pl.BlockSpec((1, rows, LANES), lambda i: (i, 0, 0))
    out = jax.ShapeDtypeStruct((b, rows, LANES), F32)
    cos, sin = pl.pallas_call(
        _rope_kernel,
        grid=(b,),
        in_specs=[spec, _const_spec((1, LANES))],
        out_specs=[spec, spec],
        out_shape=[out, out],
        compiler_params=_params(("parallel",)),
        name="rope_tables",
    )(pos, invf)
    lane_tile = lambda t: jnp.tile(t.reshape(b, s, ROPE_HALF), (1, 1, per_row))
    return lane_tile(cos), lane_tile(sin)


def _modulated_norm(x, g, shift, scale):
    ms = jnp.mean(x * x, axis=-1, keepdims=True)
    return (x * lax.rsqrt(ms + EPS) * g) * (1.0 + scale) + shift


def _group_mean_sq(y, r_ref):
    return _dot((y * y).astype(BF16), r_ref[...])


def _in_proj_kernel(x_ref, mod_ref, g_ref, w_ref, r_ref, gqk_ref, cos_ref, sin_ref,
                    dq_ref, dk_ref, dv_ref, lq_ref, lk_ref, lv_ref):
    d = x_ref.shape[-1]
    mod = mod_ref[0]
    h = _modulated_norm(x_ref[0], g_ref[...], mod[:, 0:d], mod[:, d:2 * d]).astype(BF16)
    dim = lax.broadcasted_iota(jnp.int32, cos_ref.shape[1:], 1) % HEAD_DIM
    first_half = dim < ROPE_HALF
    rope_c = jnp.where(dim < ROPE_DIM, cos_ref[0], 1.0)
    rope_s = jnp.where(first_half, -sin_ref[0], jnp.where(dim < ROPE_DIM, sin_ref[0], 0.0))
    width = dq_ref.shape[-1]
    for sec, o_ref in enumerate((dq_ref, dk_ref, dv_ref, lq_ref, lk_ref, lv_ref)):
        p = _dot(h, w_ref[:, sec * width:(sec + 1) * width])
        if o_ref is dv_ref or o_ref is lv_ref:
            o_ref[0] = p.astype(BF16)
            continue
        for c in range(width // MXU_DIM):
            col = sec * width + c * MXU_DIM
            xc = p[:, c * MXU_DIM:(c + 1) * MXU_DIM]
            y = xc * lax.rsqrt(_group_mean_sq(xc, r_ref) + EPS) * gqk_ref[:, col:col + MXU_DIM]
            for t in range(MXU_DIM // LANES):
                yt = y[:, t * LANES:(t + 1) * LANES]
                partner = jnp.where(first_half, pltpu.roll(yt, LANES - ROPE_HALF, 1), pltpu.roll(yt, ROPE_HALF, 1))
                out_col = c * MXU_DIM + t * LANES
                o_ref[0, :, out_col:out_col + LANES] = (yt * rope_c + partner * rope_s).astype(BF16)


def _in_proj_call(x, mod_l, g, w, rmat, gqk, cos_t, sin_t):
    b, s, d = x.shape
    tm = IN_PROJ_TILE
    tok = lambda w: pl.BlockSpec((1, tm, w), lambda i, j: (i, j, 0))
    width = w.shape[1] // 6
    out = jax.ShapeDtypeStruct((b, s, width), BF16)
    return pl.pallas_call(
        _in_proj_kernel,
        grid=(b, s // tm),
        in_specs=[
            tok(d),
            pl.BlockSpec((1, 1, mod_l.shape[-1]), lambda i, j: (i, 0, 0)),
            _const_spec((1, d)),
            _const_spec(w.shape, True), _const_spec(rmat.shape), _const_spec(gqk.shape),
            tok(LANES), tok(LANES),
        ],
        out_specs=[tok(width)] * 6,
        out_shape=[out] * 6,
        compiler_params=_params(("parallel", "parallel")),
        name="in_proj",
    )(x, mod_l, g, w, rmat, gqk, cos_t, sin_t)


def _diff_attn_kernel(bound_ref, q_ref, k_ref, v_ref, lam_ref, g_ref, o_ref):
    tq = DIFF_Q_BLOCK
    seq = k_ref.shape[1]
    nq = seq // tq
    heads = q_ref.shape[2] // LANES
    step_id = pl.program_id(2)
    lanes = lambda hh: slice(hh * LANES, (hh + 1) * LANES)

    def stacked_q(hh, q0):
        q = q_ref[0, pl.ds(q0, tq), lanes(hh)]
        lane = lax.broadcasted_iota(jnp.int32, q.shape, 1)
        zero = jnp.zeros_like(q)
        return jnp.concatenate([jnp.where(lane < HEAD_DIM, q, zero), jnp.where(lane >= HEAD_DIM, q, zero)], axis=0)

    def causal(s, k0):
        qry = lax.broadcasted_iota(jnp.int32, s.shape, 0) % tq
        key = lax.broadcasted_iota(jnp.int32, s.shape, 1) + k0
        return jnp.where(key <= qry, s, NEG_INF)

    def v_ext(hh, k0, n):
        return jnp.concatenate([v_ref[0, pl.ds(k0, n), lanes(hh)], jnp.ones((n, LANES), BF16)], axis=1)

    def finish(hh, q0, acc):
        o1 = acc[:tq, :LANES] / acc[:tq, LANES:]
        o2 = acc[tq:, :LANES] / acc[tq:, LANES:]
        o = o1 - lam_ref[0:1, 0:1] * o2
        ms = jnp.mean(o * o, axis=-1, keepdims=True)
        o_ref[0, pl.ds(q0, tq), lanes(hh)] = (o * lax.rsqrt(ms + EPS) * g_ref[...]).astype(o_ref.dtype)

    def unshifted(blk):
        kv = (blk + 1) * tq
        for hh in range(heads):
            s = causal(_dot_nt(stacked_q(hh, blk * tq), k_ref[0, :kv, lanes(hh)]), -blk * tq)
            finish(hh, blk * tq, _dot(jnp.exp2(s).astype(BF16), v_ext(hh, 0, kv)))

    def shifted(blk):
        q0 = pl.multiple_of(blk * tq, tq)
        for hh in range(heads):
            qb = stacked_q(hh, q0)

            def step(j, carry, diagonal, hh=hh, qb=qb):
                m, acc = carry
                k0 = pl.multiple_of(j * tq, tq)
                s = _dot_nt(qb, k_ref[0, pl.ds(k0, tq), lanes(hh)])
                if diagonal:
                    s = causal(s, 0)
                m_new = jnp.maximum(m, jnp.max(s, axis=-1, keepdims=True))
                acc = jnp.exp2(m - m_new) * acc + _dot(jnp.exp2(s - m_new).astype(BF16), v_ext(hh, k0, tq))
                return m_new, acc

            carry = (jnp.full((2 * tq, 1), NEG_INF, F32), jnp.zeros((2 * tq, 2 * LANES), F32))
            carry = lax.fori_loop(0, blk, lambda j, c, step=step: step(j, c, False), carry)
            finish(hh, q0, step(blk, carry, True)[1])

    safe = bound_ref[0] <= SAFE_LOGIT
    for pair in range(nq // 2):
        @pl.when(jnp.logical_and(step_id == pair, safe))
        def _(pair=pair):
            unshifted(pair)
            unshifted(nq - 1 - pair)

    @pl.when(jnp.logical_not(safe))
    def _():
        def one_block(which, _):
            shifted(jnp.where(which == 0, step_id, nq - 1 - step_id))
            return 0
        lax.fori_loop(0, 2, one_block, 0)


def _diff_attn_call(bound, dq, dk, dv, lam_l, g_sub):
    b, s, w = dq.shape
    cols = DIFF_HEADS_PER_STEP * LANES
    rows = pl.BlockSpec((1, s, cols), lambda i, h, j: (i, 0, h))
    return pl.pallas_call(
        _diff_attn_kernel,
        grid=(b, w // cols, s // DIFF_Q_BLOCK // 2),
        in_specs=[_SMEM_SPEC, rows, rows, rows, _const_spec((1, LANES)), _const_spec((1, LANES))],
        out_specs=rows,
        out_shape=jax.ShapeDtypeStruct((b, s, w), BF16),
        compiler_params=_params(("parallel", "parallel", "arbitrary")),
        name="diff_attn",
    )(bound, dq, dk, dv, lam_l, g_sub)


def _rows(start, size, stride):
    return pl.ds(start, size, stride=stride) if stride > 1 else pl.ds(start, size)


def _dil_lane_block(q_ref, k_ref, v_ref, r_ref, g_ref, o_ref,
                    stage, stage_mid, qp, kp, vm, num, den, mx, num_mid, den_mid):
    seq = q_ref.shape[1]
    blk = DIL_BLOCK
    head0 = lax.broadcasted_iota(jnp.int32, (blk, LANES), 1) < HEAD_DIM
    strides = tuple(stride for _, stride in DILATED_PAIRS)
    mid = strides[1]
    mid_len = seq // mid

    def mid_rows(r, n_rows):
        return pl.ds((r % mid) * mid_len + r // mid, n_rows, stride=mid)

    def stage_rows(di, dst_rows, q, k, v):
        lane0 = lax.broadcasted_iota(jnp.int32, v.shape, 1) < HEAD_DIM
        one = jnp.ones_like(v)
        if di > 0:
            qp[di - 1, dst_rows, :] = q
            kp[di - 1, dst_rows, :] = k
        vm[di, 0, dst_rows, :] = jnp.where(lane0, v, one)
        vm[di, 1, dst_rows, :] = jnp.where(lane0, one, v)

    def stage_all():
        stage_rows(0, slice(None), None, None, v_ref[0])
        for j, ref in enumerate((q_ref, k_ref, v_ref)):
            stage[j] = ref[0].astype(F32)
            for r in range(mid):
                stage_mid[j, r * mid_len:(r + 1) * mid_len, :] = stage[j, pl.ds(r, mid_len, stride=mid), :]
        stage_rows(1, slice(None), *(stage_mid[j].astype(BF16) for j in range(3)))
        last_len = seq // strides[2]
        for r in range(strides[2]):
            stage_rows(2, slice(r * last_len, (r + 1) * last_len),
                       *(stage_mid[j, mid_rows(r, last_len), :].astype(BF16) for j in range(3)))

    def q_rows(di, rows):
        return q_ref[0, rows, :] if di == 0 else qp[di - 1, rows, :]

    def k_rows(di, rows):
        return k_ref[0, rows, :] if di == 0 else kp[di - 1, rows, :]

    def window(ctx):
        qq = lax.broadcasted_iota(jnp.int32, (2 * blk, ctx), 0) % blk
        kk = lax.broadcasted_iota(jnp.int32, (2 * blk, ctx), 1)
        return (kk <= qq) if ctx == blk else ((kk >= qq) & (kk <= qq + blk))

    def block(di, r, n, mode):
        stride = strides[di]
        first = di == 0
        q0 = r * (seq // stride) + n * blk
        k0 = q0 - blk if n > 0 else q0
        if not isinstance(r, int):
            q0, k0 = pl.multiple_of(q0, blk), pl.multiple_of(k0, blk)
        ctx = 2 * blk if n > 0 else blk
        out_rows = _rows(r + n * blk * stride, blk, stride)
        q = q_rows(di, pl.ds(q0, blk))
        zero = jnp.zeros_like(q)
        qb = jnp.concatenate([jnp.where(head0, q, zero), jnp.where(head0, zero, q)], axis=0)
        s = _dot_nt(qb, k_rows(di, pl.ds(k0, ctx)))
        s = jnp.where(window(ctx), s, NEG_INF)
        if mode == "max":
            m = jnp.max(s, axis=-1, keepdims=True)
            m = jnp.where(head0, m[:blk], m[blk:])
            mx[out_rows, :] = m if first else jnp.maximum(mx[out_rows, :], m)
            return
        if mode == "shifted":
            m = mx[out_rows, :]
            s = s - jnp.concatenate([m[:, 0:1], m[:, HEAD_DIM:HEAD_DIM + 1]], axis=0)
        p = jnp.exp2(s).astype(BF16)
        out0 = _dot(p[:blk], vm[di, 0, pl.ds(k0, ctx), :])
        out1 = _dot(p[blk:], vm[di, 1, pl.ds(k0, ctx), :])
        n_blk = jnp.where(head0, out0, out1)
        d_blk = jnp.where(head0, out1, out0)
        if di == 0:
            num[out_rows, :] = n_blk
            den[out_rows, :] = d_blk
        elif di == 2:
            num_mid[mid_rows(r, blk), :] = n_blk
            den_mid[mid_rows(r, blk), :] = d_blk
        else:
            num[out_rows, :] = num[out_rows, :] + (n_blk + num_mid[pl.ds(q0, blk), :])
            den[out_rows, :] = den[out_rows, :] + (d_blk + den_mid[pl.ds(q0, blk), :])

    def sweep(mode, straight_line):
        for di in (0, 2, 1):
            stride = strides[di]
            nb = seq // stride // blk
            if straight_line or stride == 1:
                for r in range(stride):
                    for n in range(nb):
                        block(di, r, n, mode)
            else:
                def subsequence(r, _, di=di, nb=nb):
                    for n in range(nb):
                        block(di, r, n, mode)
                    return 0
                lax.fori_loop(0, stride, subsequence, 0)

    def finish():
        o = num[...] / pltpu.roll(den[...], HEAD_DIM, 1)
        ms = _dot((o * o).astype(BF16), r_ref[...])
        o_ref[0] = (o * lax.rsqrt(ms + EPS) * g_ref[...]).astype(o_ref.dtype)

    return stage_all, sweep, finish


def _dil_attn_kernel(bound_ref, q_ref, k_ref, v_ref, r_ref, g_ref, o_ref, *scratch):
    parts = []
    for hp in range(q_ref.shape[2] // LANES):
        cols = pl.ds(hp * LANES, LANES)
        parts.append(_dil_lane_block(q_ref.at[:, :, cols], k_ref.at[:, :, cols], v_ref.at[:, :, cols], r_ref,
                                     g_ref.at[:, cols], o_ref.at[:, :, cols], *(ref.at[hp] for ref in scratch)))
    safe = bound_ref[0] <= SAFE_LOGIT

    @pl.when(safe)
    def _():
        for stage_all, _, _ in parts:
            stage_all()
        for _, sweep, _ in parts:
            sweep("plain", True)
        for _, _, finish in parts:
            finish()

    @pl.when(jnp.logical_not(safe))
    def _():
        for stage_all, sweep, finish in parts:
            stage_all()
            sweep("max", False)
            sweep("shifted", False)
            finish()


def _dil_attn_call(bound, lq, lk, lv, rmat, gd):
    b, s, w = lq.shape
    per_step = DIL_LANE_BLOCKS_PER_STEP
    cols = per_step * LANES
    spec = pl.BlockSpec((1, s, cols), lambda i, h: (i, 0, h))
    f32_rows = pltpu.VMEM((per_step, s, LANES), F32)
    npairs = len(DILATED_PAIRS)
    strides = tuple(stride for _, stride in DILATED_PAIRS)
    assert strides == (1, strides[1], strides[1] ** 2) and s // strides[2] == DIL_BLOCK
    assert all(window // stride == DIL_BLOCK for window, stride in DILATED_PAIRS)
    return pl.pallas_call(
        _dil_attn_kernel,
        grid=(b, w // cols),
        in_specs=[_SMEM_SPEC, spec, spec, spec, _const_spec((LANES, LANES)),
                  pl.BlockSpec((1, cols), lambda i, h: (0, h))],
        out_specs=spec,
        out_shape=jax.ShapeDtypeStruct((b, s, w), BF16),
        scratch_shapes=[pltpu.VMEM((per_step, 3, s, LANES), F32),
                        pltpu.VMEM((per_step, 3, s, LANES), F32),
                        pltpu.VMEM((per_step, npairs - 1, s, LANES), BF16),
                        pltpu.VMEM((per_step, npairs - 1, s, LANES), BF16),
                        pltpu.VMEM((per_step, npairs, 2, s, LANES), BF16),
                        f32_rows, f32_rows, f32_rows, f32_rows, f32_rows],
        compiler_params=_params(("parallel", "parallel")),
        name="dil_attn",
    )(bound, lq, lk, lv, rmat, gd)


def _mix_ffn_kernel(x_ref, mod_ref, oa_ref, ob_ref, wo_ref, g_ref, wu_ref, cw_ref, cb_ref, wd_ref,
                    xo_ref, tail_ref):
    d = x_ref.shape[-1]
    tm = x_ref.shape[1]
    d_ff = wd_ref.shape[0]
    nchunk = d_ff // FF_CHUNK
    halo = tail_ref.shape[2]
    wa = oa_ref.shape[-1]

    @pl.when(pl.program_id(1) == 0)
    def _():
        tail_ref[...] = jnp.zeros_like(tail_ref)

    mod = mod_ref[0]
    mixed = _dot(oa_ref[0], wo_ref[:wa, :]) + _dot(ob_ref[0], wo_ref[wa:, :])
    x = x_ref[0] + mod[:, 2 * d:3 * d] * mixed
    h = _modulated_norm(x, g_ref[...], mod[:, 3 * d:4 * d], mod[:, 4 * d:5 * d]).astype(BF16)
    row = lax.broadcasted_iota(jnp.int32, (halo, FF_CHUNK), 0)

    def chunk(c):
        convs = []
        for part in range(2):
            cols = slice(part * d_ff + c * FF_CHUNK, part * d_ff + (c + 1) * FF_CHUNK)
            u = _dot(h, wu_ref[:, cols])
            prev = tail_ref[c, part]
            tail_ref[c, part] = u[tm - halo:, :]
            conv = cb_ref[:, cols] + u * cw_ref[CONV_WIDTH - 1:CONV_WIDTH, cols]
            for lag in range(1, CONV_WIDTH):
                shifted = pltpu.roll(u, lag, 0)
                head = jnp.where(row < lag, pltpu.roll(prev, lag, 0), shifted[:halo])
                shifted = jnp.concatenate([head, shifted[halo:]], axis=0)
                conv = conv + shifted * cw_ref[CONV_WIDTH - 1 - lag:CONV_WIDTH - lag, cols]
            convs.append(conv)
        half_gate, val = convs
        return ((half_gate + half_gate * jnp.tanh(half_gate)) * val).astype(BF16)

    y = None
    for c0 in range(0, nchunk, FF_DOWN_GROUP):
        c1 = min(c0 + FF_DOWN_GROUP, nchunk)
        act = jnp.concatenate([chunk(c) for c in range(c0, c1)], axis=1)
        part = _dot(act, wd_ref[c0 * FF_CHUNK:c1 * FF_CHUNK, :])
        y = part if y is None else y + part
    xo_ref[0] = x + mod[:, 5 * d:6 * d] * y


def _mix_ffn_call(x, mod_l, oa, ob, wo, g, wu, cw, cb, wd):
    b, s, d = x.shape
    tm = FFN_TILE
    tok = lambda w: pl.BlockSpec((1, tm, w), lambda i, j: (i, j, 0))
    nchunk = wd.shape[0] // FF_CHUNK
    return pl.pallas_call(
        _mix_ffn_kernel,
        grid=(b, s // tm),
        in_specs=[tok(d), pl.BlockSpec((1, 1, mod_l.shape[-1]), lambda i, j: (i, 0, 0)),
                  tok(oa.shape[-1]), tok(ob.shape[-1]), _const_spec(wo.shape, True), _const_spec((1, d)),
                  _const_spec(wu.shape, True), _const_spec(cw.shape), _const_spec(cb.shape),
                  _const_spec(wd.shape, True)],
        out_specs=tok(d),
        out_shape=jax.ShapeDtypeStruct(x.shape, x.dtype),
        scratch_shapes=[pltpu.VMEM((nchunk, 2, SUBLANES, FF_CHUNK), F32)],
        compiler_params=_params(("parallel", "arbitrary")),
        name="mix_ffn",
    )(x, mod_l, oa, ob, wo, g, wu, cw, cb, wd)


def _group_mean_matrix(n):
    idx = np.arange(n) // HEAD_DIM
    return jnp.asarray((idx[:, None] == idx[None, :]).astype(np.float32) / HEAD_DIM, dtype=BF16)


def _logit_bound(g_q, g_k):
    return (HEAD_DIM * QK_SCALE * jnp.max(jnp.abs(g_q)) * jnp.max(jnp.abs(g_k))).reshape(1).astype(F32)


def kernel(x, c, positions, g_mix, g_ffn, w_ada, b_ada, w_in, w_out, diff_q_g, diff_k_g, lam_q1, lam_k1, lam_q2, lam_k2, diff_subln_g, dil_q_g, dil_k_g, dil_out_g, w_up, conv_w, conv_b, w_down):
    depth, d, _ = w_in.shape
    b = x.shape[0]
    d_ff = w_down.shape[1]

    lam_init = np.array([0.8 - 0.6 * math.exp(-0.3 * l) for l in range(depth)], np.float32)
    lam_init_tile = jnp.asarray(np.broadcast_to(lam_init[:, None, None], (depth, 1, LANES)))
    mod, lam = _mod_call(c, w_ada, b_ada, lam_q1, lam_k1, lam_q2, lam_k2, lam_init_tile)
    cos_t, sin_t = _rope_call(positions)
    rmat_qk = _group_mean_matrix(MXU_DIM)
    rmat_out = _group_mean_matrix(LANES)
    gate_half = jnp.concatenate([jnp.full((1, d_ff), 0.5, F32), jnp.ones((1, d_ff), F32)], axis=1)
    reps = DIFF_WIDTH // HEAD_DIM
    ones = jnp.ones((DIFF_WIDTH,), F32)

    for l in range(depth):
        gqk = jnp.concatenate([jnp.tile(diff_q_g[l], reps) * QK_SCALE, jnp.tile(diff_k_g[l], reps), ones,
                               jnp.tile(dil_q_g[l], reps) * QK_SCALE, jnp.tile(dil_k_g[l], reps), ones]).reshape(1, -1)
        mod_l = mod[l].reshape(b, 1, -1)
        dq, dk, dv, lq, lk, lv = _in_proj_call(x, mod_l, g_mix[l].reshape(1, d), w_in[l].astype(BF16), rmat_qk, gqk,
                                               cos_t, sin_t)
        g_sub = (diff_subln_g[l] * (1.0 - float(lam_init[l]))).reshape(1, DIFF_V_DIM)
        oa = _diff_attn_call(_logit_bound(diff_q_g[l], diff_k_g[l]), dq, dk, dv, lam[l], g_sub)
        gd = jnp.tile(dil_out_g[l], N_DIL_HEADS).reshape(1, DIL_WIDTH)
        ob = _dil_attn_call(_logit_bound(dil_q_g[l], dil_k_g[l]), lq, lk, lv, rmat_out, gd)
        x = _mix_ffn_call(x, mod_l, oa, ob, w_out[l].astype(BF16), g_ffn[l].reshape(1, d), w_up[l].astype(BF16),
                          conv_w[l] * gate_half, conv_b[l].reshape(1, -1) * gate_half, w_down[l].astype(BF16))
    return x
```

```python
import math

import jax
import jax.numpy as jnp
import numpy as np
from jax import lax
from jax.experimental import pallas as pl
from jax.experimental.pallas import tpu as pltpu

HEAD_DIM = 64
N_DIFF_HEADS = 4
DIFF_V_DIM = 2 * HEAD_DIM
DIFF_WIDTH = N_DIFF_HEADS * DIFF_V_DIM
N_DIL_HEADS = 8
DIL_WIDTH = N_DIL_HEADS * HEAD_DIM
DILATED_PAIRS = ((128, 1), (512, 4), (2048, 16))
ROPE_THETA = 500000.0
ROPE_DIM = HEAD_DIM // 4
ROPE_HALF = ROPE_DIM // 2
CONV_WIDTH = 3
EPS = 1e-6
NEG_INF = -1e30
LOG2E = 1.4426950408889634
QK_SCALE = HEAD_DIM ** -0.5 * LOG2E

LANES = 128
SUBLANES = 8
MXU_DIM = 256
DIFF_Q_BLOCK = 256
FFN_TILE = 512
IN_PROJ_TILE = 1024
DIFF_HEADS_PER_STEP = 4
DIL_BLOCK = 128
DIL_LANE_BLOCKS_PER_STEP = 2
FF_CHUNK = 256
FF_DOWN_GROUP = 11
SAFE_LOGIT = 64.0
VMEM_LIMIT = 56 * 1024 * 1024

F32 = jnp.float32
BF16 = jnp.bfloat16


def _dot(a, b):
    return jnp.dot(a, b, preferred_element_type=F32)


def _dot_nt(a, b):
    return lax.dot_general(a, b, (((1,), (1,)), ((), ())), preferred_element_type=F32)


def _split_bf16(x):
    hi = x.astype(BF16)
    lo = (x - hi.astype(F32)).astype(BF16)
    return hi, lo


def _params(sem, vmem=VMEM_LIMIT):
    return pltpu.CompilerParams(dimension_semantics=sem, vmem_limit_bytes=vmem)


def _const_spec(shape, single_buffer=False):
    mode = pl.Buffered(1) if single_buffer else None
    return pl.BlockSpec(shape, lambda *_: (0,) * len(shape), pipeline_mode=mode)


_SMEM_SPEC = pl.BlockSpec(memory_space=pltpu.SMEM)


def _mod_kernel(c_ref, w_ref, b_ref, q1_ref, k1_ref, q2_ref, k2_ref, li_ref, mod_ref, lam_ref):
    c = c_ref[...]
    cond = c * (1.0 / (1.0 + jnp.exp(-c)))
    ch, cl = _split_bf16(cond)
    wh, wl = _split_bf16(w_ref[0])
    mod_ref[0] = _dot(ch, wh) + (_dot(ch, wl) + _dot(cl, wh)) + b_ref[0]
    s1 = jnp.sum(q1_ref[0] * k1_ref[0], axis=-1, keepdims=True)
    s2 = jnp.sum(q2_ref[0] * k2_ref[0], axis=-1, keepdims=True)
    lam_ref[0] = (jnp.exp(s1) - jnp.exp(s2)) + li_ref[0]


def _mod_call(c, w_ada, b_ada, lam_q1, lam_k1, lam_q2, lam_k2, lam_init):
    depth, d, d6 = w_ada.shape
    b = c.shape[0]
    nj = d6 // d
    vec = lambda a: a.reshape(depth, 1, HEAD_DIM)
    vspec = pl.BlockSpec((1, 1, HEAD_DIM), lambda l, j: (l, 0, 0))
    return pl.pallas_call(
        _mod_kernel,
        grid=(depth, nj),
        in_specs=[
            pl.BlockSpec((b, d), lambda l, j: (0, 0)),
            pl.BlockSpec((1, d, d), lambda l, j: (l, 0, j)),
            pl.BlockSpec((1, 1, d), lambda l, j: (l, 0, j)),
            vspec, vspec, vspec, vspec,
            pl.BlockSpec((1, 1, LANES), lambda l, j: (l, 0, 0)),
        ],
        out_specs=[
            pl.BlockSpec((1, b, d), lambda l, j: (l, 0, j)),
            pl.BlockSpec((1, 1, LANES), lambda l, j: (l, 0, 0)),
        ],
        out_shape=[
            jax.ShapeDtypeStruct((depth, b, d6), F32),
            jax.ShapeDtypeStruct((depth, 1, LANES), F32),
        ],
        compiler_params=_params(("arbitrary", "arbitrary")),
        name="adaln_mod",
    )(c, w_ada, b_ada.reshape(depth, 1, d6), vec(lam_q1), vec(lam_k1), vec(lam_q2), vec(lam_k2), lam_init)


def _rope_kernel(pos_ref, invf_ref, c_ref, s_ref):
    ang = pos_ref[0].astype(F32) * invf_ref[...]
    c_ref[0] = jnp.cos(ang)
    s_ref[0] = jnp.sin(ang)


def _rope_call(positions):
    b, s = positions.shape
    per_row = LANES // ROPE_HALF
    rows = s // per_row
    inv_freq = ROPE_THETA ** (-jnp.arange(0, ROPE_DIM, 2, dtype=F32) / ROPE_DIM)
    invf = jnp.tile(inv_freq, per_row).reshape(1, LANES)
    pos = jnp.repeat(positions.reshape(b, rows, per_row), ROPE_HALF, axis=-1)
    spec = pl.BlockSpec((1, rows, LANES), lambda i: (i, 0, 0))
    out = jax.ShapeDtypeStruct((b, rows, LANES), F32)
    cos, sin = pl.pallas_call(
        _rope_kernel,
        grid=(b,),
        in_specs=[spec, _const_spec((1, LANES))],
        out_specs=[spec, spec],
        out_shape=[out, out],
        compiler_params=_params(("parallel",)),
        name="rope_tables",
    )(pos, invf)
    lane_tile = lambda t: jnp.tile(t.reshape(b, s, ROPE_HALF), (1, 1, per_row))
    return lane_tile(cos), lane_tile(sin)


def _modulated_norm(x, g, shift, scale):
    ms = jnp.mean(x * x, axis=-1, keepdims=True)
    return (x * lax.rsqrt(ms + EPS) * g) * (1.0 + scale) + shift


def _group_mean_sq(y, r_ref):
    return _dot((y * y).astype(BF16), r_ref[...])


def _in_proj_kernel(x_ref, mod_ref, g_ref, w_ref, r_ref, gqk_ref, cos_ref, sin_ref,
                    dq_ref, dk_ref, dv_ref, lq_ref, lk_ref, lv_ref):
    d = x_ref.shape[-1]
    mod = mod_ref[0]
    h = _modulated_norm(x_ref[0], g_ref[...], mod[:, 0:d], mod[:, d:2 * d]).astype(BF16)
    dim = lax.broadcasted_iota(jnp.int32, cos_ref.shape[1:], 1) % HEAD_DIM
    first_half = dim < ROPE_HALF
    rope_c = jnp.where(dim < ROPE_DIM, cos_ref[0], 1.0)
    rope_s = jnp.where(first_half, -sin_ref[0], jnp.where(dim < ROPE_DIM, sin_ref[0], 0.0))
    width = dq_ref.shape[-1]

    def put(o_ref, col, value):
        n = value.shape[1]
        if len(o_ref.shape) == 3:
            o_ref[0, :, col:col + n] = value
            return
        piece = min(n, o_ref.shape[-1])
        for c0 in range(0, n, piece):
            group, off = divmod(col + c0, o_ref.shape[-1])
            o_ref[0, group, :, off:off + piece] = value[:, c0:c0 + piece]

    for sec, o_ref in enumerate((dq_ref, dk_ref, dv_ref, lq_ref, lk_ref, lv_ref)):
        p = _dot(h, w_ref[:, sec * width:(sec + 1) * width])
        if o_ref is dv_ref or o_ref is lv_ref:
            put(o_ref, 0, p.astype(BF16))
            continue
        for c in range(width // MXU_DIM):
            col = sec * width + c * MXU_DIM
            xc = p[:, c * MXU_DIM:(c + 1) * MXU_DIM]
            y = xc * lax.rsqrt(_group_mean_sq(xc, r_ref) + EPS) * gqk_ref[:, col:col + MXU_DIM]
            for t in range(MXU_DIM // LANES):
                yt = y[:, t * LANES:(t + 1) * LANES]
                partner = jnp.where(first_half, pltpu.roll(yt, LANES - ROPE_HALF, 1), pltpu.roll(yt, ROPE_HALF, 1))
                out_col = c * MXU_DIM + t * LANES
                put(o_ref, out_col, (yt * rope_c + partner * rope_s).astype(BF16))


def _in_proj_call(x, mod_l, g, w, rmat, gqk, cos_t, sin_t):
    b, s, d = x.shape
    tm = IN_PROJ_TILE
    tok = lambda w: pl.BlockSpec((1, tm, w), lambda i, j: (i, j, 0))
    width = w.shape[1] // 6
    out = jax.ShapeDtypeStruct((b, s, width), BF16)
    gw = DIL_LANE_BLOCKS_PER_STEP * LANES
    grouped = jax.ShapeDtypeStruct((b, width // gw, s, gw), BF16)
    grouped_spec = pl.BlockSpec((1, width // gw, tm, gw), lambda i, j: (i, 0, j, 0))
    return pl.pallas_call(
        _in_proj_kernel,
        grid=(b, s // tm),
        in_specs=[
            tok(d),
            pl.BlockSpec((1, 1, mod_l.shape[-1]), lambda i, j: (i, 0, 0)),
            _const_spec((1, d)),
            _const_spec(w.shape, True), _const_spec(rmat.shape), _const_spec(gqk.shape),
            tok(LANES), tok(LANES),
        ],
        out_specs=[tok(width)] * 3 + [grouped_spec] * 3,
        out_shape=[out] * 3 + [grouped] * 3,
        compiler_params=_params(("parallel", "parallel")),
        name="in_proj",
    )(x, mod_l, g, w, rmat, gqk, cos_t, sin_t)


def _diff_attn_kernel(bound_ref, q_ref, k_ref, v_ref, lam_ref, g_ref, o_ref):
    tq = DIFF_Q_BLOCK
    seq = k_ref.shape[1]
    nq = seq // tq
    heads = q_ref.shape[2] // LANES
    step_id = pl.program_id(2)
    lanes = lambda hh: slice(hh * LANES, (hh + 1) * LANES)

    def stacked_q(hh, q0):
        q = q_ref[0, pl.ds(q0, tq), lanes(hh)]
        lane = lax.broadcasted_iota(jnp.int32, q.shape, 1)
        zero = jnp.zeros_like(q)
        return jnp.concatenate([jnp.where(lane < HEAD_DIM, q, zero), jnp.where(lane >= HEAD_DIM, q, zero)], axis=0)

    def causal(s, k0):
        qry = lax.broadcasted_iota(jnp.int32, s.shape, 0) % tq
        key = lax.broadcasted_iota(jnp.int32, s.shape, 1) + k0
        return jnp.where(key <= qry, s, NEG_INF)

    def v_ext(hh, k0, n):
        return jnp.concatenate([v_ref[0, pl.ds(k0, n), lanes(hh)], jnp.ones((n, LANES), BF16)], axis=1)

    def finish(hh, q0, acc):
        o1 = acc[:tq, :LANES] / acc[:tq, LANES:]
        o2 = acc[tq:, :LANES] / acc[tq:, LANES:]
        o = o1 - lam_ref[0:1, 0:1] * o2
        ms = jnp.mean(o * o, axis=-1, keepdims=True)
        o_ref[0, pl.ds(q0, tq), lanes(hh)] = (o * lax.rsqrt(ms + EPS) * g_ref[...]).astype(o_ref.dtype)

    def unshifted(blk):
        kv = (blk + 1) * tq
        for hh in range(heads):
            s = causal(_dot_nt(stacked_q(hh, blk * tq), k_ref[0, :kv, lanes(hh)]), -blk * tq)
            finish(hh, blk * tq, _dot(jnp.exp2(s).astype(BF16), v_ext(hh, 0, kv)))

    def shifted(blk):
        q0 = pl.multiple_of(blk * tq, tq)
        for hh in range(heads):
            qb = stacked_q(hh, q0)

            def step(j, carry, diagonal, hh=hh, qb=qb):
                m, acc = carry
                k0 = pl.multiple_of(j * tq, tq)
                s = _dot_nt(qb, k_ref[0, pl.ds(k0, tq), lanes(hh)])
                if diagonal:
                    s = causal(s, 0)
                m_new = jnp.maximum(m, jnp.max(s, axis=-1, keepdims=True))
                acc = jnp.exp2(m - m_new) * acc + _dot(jnp.exp2(s - m_new).astype(BF16), v_ext(hh, k0, tq))
                return m_new, acc

            carry = (jnp.full((2 * tq, 1), NEG_INF, F32), jnp.zeros((2 * tq, 2 * LANES), F32))
            carry = lax.fori_loop(0, blk, lambda j, c, step=step: step(j, c, False), carry)
            finish(hh, q0, step(blk, carry, True)[1])

    safe = bound_ref[0] <= SAFE_LOGIT
    for pair in range(nq // 2):
        @pl.when(jnp.logical_and(step_id == pair, safe))
        def _(pair=pair):
            unshifted(pair)
            unshifted(nq - 1 - pair)

    @pl.when(jnp.logical_not(safe))
    def _():
        def one_block(which, _):
            shifted(jnp.where(which == 0, step_id, nq - 1 - step_id))
            return 0
        lax.fori_loop(0, 2, one_block, 0)


def _diff_attn_call(bound, dq, dk, dv, lam_l, g_sub):
    b, s, w = dq.shape
    cols = DIFF_HEADS_PER_STEP * LANES
    rows = pl.BlockSpec((1, s, cols), lambda i, h, j: (i, 0, h))
    return pl.pallas_call(
        _diff_attn_kernel,
        grid=(b, w // cols, s // DIFF_Q_BLOCK // 2),
        in_specs=[_SMEM_SPEC, rows, rows, rows, _const_spec((1, LANES)), _const_spec((1, LANES))],
        out_specs=rows,
        out_shape=jax.ShapeDtypeStruct((b, s, w), BF16),
        compiler_params=_params(("parallel", "parallel", "arbitrary")),
        name="diff_attn",
    )(bound, dq, dk, dv, lam_l, g_sub)


def _rows(start, size, stride):
    return pl.ds(start, size, stride=stride) if stride > 1 else pl.ds(start, size)


def _dil_lane_block(q_ref, k_ref, v_ref, r_ref, g_ref, o_ref,
                    stage, stage_mid, qp, kp, vm, num, den, mx, num_mid, den_mid):
    seq = q_ref.shape[1]
    blk = DIL_BLOCK
    head0 = lax.broadcasted_iota(jnp.int32, (blk, LANES), 1) < HEAD_DIM
    strides = tuple(stride for _, stride in DILATED_PAIRS)
    mid = strides[1]
    mid_len = seq // mid

    def mid_rows(r, n_rows):
        return pl.ds((r % mid) * mid_len + r // mid, n_rows, stride=mid)

    def stage_rows(di, dst_rows, q, k, v):
        lane0 = lax.broadcasted_iota(jnp.int32, v.shape, 1) < HEAD_DIM
        one = jnp.ones_like(v)
        if di > 0:
            qp[di - 1, dst_rows, :] = q
            kp[di - 1, dst_rows, :] = k
        vm[di, 0, dst_rows, :] = jnp.where(lane0, v, one)
        vm[di, 1, dst_rows, :] = jnp.where(lane0, one, v)

    def stage_all():
        stage_rows(0, slice(None), None, None, v_ref[0])
        for j, ref in enumerate((q_ref, k_ref, v_ref)):
            stage[j] = ref[0].astype(F32)
            for r in range(mid):
                stage_mid[j, r * mid_len:(r + 1) * mid_len, :] = stage[j, pl.ds(r, mid_len, stride=mid), :]
        stage_rows(1, slice(None), *(stage_mid[j].astype(BF16) for j in range(3)))
        last_len = seq // strides[2]
        for r in range(strides[2]):
            stage_rows(2, slice(r * last_len, (r + 1) * last_len),
                       *(stage_mid[j, mid_rows(r, last_len), :].astype(BF16) for j in range(3)))

    def q_rows(di, rows):
        return q_ref[0, rows, :] if di == 0 else qp[di - 1, rows, :]

    def k_rows(di, rows):
        return k_ref[0, rows, :] if di == 0 else kp[di - 1, rows, :]

    def window(ctx):
        qq = lax.broadcasted_iota(jnp.int32, (2 * blk, ctx), 0) % blk
        kk = lax.broadcasted_iota(jnp.int32, (2 * blk, ctx), 1)
        return (kk <= qq) if ctx == blk else ((kk >= qq) & (kk <= qq + blk))

    def block(di, r, n, mode):
        stride = strides[di]
        first = di == 0
        q0 = r * (seq // stride) + n * blk
        k0 = q0 - blk if n > 0 else q0
        if not isinstance(r, int):
            q0, k0 = pl.multiple_of(q0, blk), pl.multiple_of(k0, blk)
        ctx = 2 * blk if n > 0 else blk
        out_rows = _rows(r + n * blk * stride, blk, stride)
        q = q_rows(di, pl.ds(q0, blk))
        zero = jnp.zeros_like(q)
        qb = jnp.concatenate([jnp.where(head0, q, zero), jnp.where(head0, zero, q)], axis=0)
        s = _dot_nt(qb, k_rows(di, pl.ds(k0, ctx)))
        s = jnp.where(window(ctx), s, NEG_INF)
        if mode == "max":
            m = jnp.max(s, axis=-1, keepdims=True)
            m = jnp.where(head0, m[:blk], m[blk:])
            mx[out_rows, :] = m if first else jnp.maximum(mx[out_rows, :], m)
            return
        if mode == "shifted":
            m = mx[out_rows, :]
            s = s - jnp.concatenate([m[:, 0:1], m[:, HEAD_DIM:HEAD_DIM + 1]], axis=0)
        p = jnp.exp2(s).astype(BF16)
        out0 = _dot(p[:blk], vm[di, 0, pl.ds(k0, ctx), :])
        out1 = _dot(p[blk:], vm[di, 1, pl.ds(k0, ctx), :])
        n_blk = jnp.where(head0, out0, out1)
        d_blk = jnp.where(head0, out1, out0)
        if di == 0:
            num[out_rows, :] = n_blk
            den[out_rows, :] = d_blk
        elif di == 2:
            num_mid[mid_rows(r, blk), :] = n_blk
            den_mid[mid_rows(r, blk), :] = d_blk
        else:
            num[out_rows, :] = num[out_rows, :] + (n_blk + num_mid[pl.ds(q0, blk), :])
            den[out_rows, :] = den[out_rows, :] + (d_blk + den_mid[pl.ds(q0, blk), :])

    def sweep(mode, straight_line):
        for di in (0, 2, 1):
            stride = strides[di]
            nb = seq // stride // blk
            if straight_line or stride == 1:
                for r in range(stride):
                    for n in range(nb):
                        block(di, r, n, mode)
            else:
                def subsequence(r, _, di=di, nb=nb):
                    for n in range(nb):
                        block(di, r, n, mode)
                    return 0
                lax.fori_loop(0, stride, subsequence, 0)

    def finish():
        o = num[...] / pltpu.roll(den[...], HEAD_DIM, 1)
        ms = _dot((o * o).astype(BF16), r_ref[...])
        o_ref[0] = (o * lax.rsqrt(ms + EPS) * g_ref[...]).astype(o_ref.dtype)

    return stage_all, sweep, finish


def _dil_attn_kernel(bound_ref, q_ref, k_ref, v_ref, r_ref, g_ref, o_ref, *scratch):
    parts = []
    for hp in range(q_ref.shape[2] // LANES):
        cols = pl.ds(hp * LANES, LANES)
        parts.append(_dil_lane_block(q_ref.at[:, :, cols], k_ref.at[:, :, cols], v_ref.at[:, :, cols], r_ref,
                                     g_ref.at[:, cols], o_ref.at[:, :, cols], *(ref.at[hp] for ref in scratch)))
    safe = bound_ref[0] <= SAFE_LOGIT

    @pl.when(safe)
    def _():
        for stage_all, _, _ in parts:
            stage_all()
        for _, sweep, _ in parts:
            sweep("plain", True)
        for _, _, finish in parts:
            finish()

    @pl.when(jnp.logical_not(safe))
    def _():
        for stage_all, sweep, finish in parts:
            stage_all()
            sweep("max", False)
            sweep("shifted", False)
            finish()


def _dil_attn_call(bound, lq, lk, lv, rmat, gd):
    b, groups, s, cols = lq.shape
    per_step = cols // LANES
    w = groups * cols
    spec = pl.BlockSpec((1, None, s, cols), lambda i, h: (i, h, 0, 0))
    f32_rows = pltpu.VMEM((per_step, s, LANES), F32)
    npairs = len(DILATED_PAIRS)
    strides = tuple(stride for _, stride in DILATED_PAIRS)
    assert strides == (1, strides[1], strides[1] ** 2) and s // strides[2] == DIL_BLOCK
    assert all(window // stride == DIL_BLOCK for window, stride in DILATED_PAIRS)
    return pl.pallas_call(
        _dil_attn_kernel,
        grid=(b, w // cols),
        in_specs=[_SMEM_SPEC, spec, spec, spec, _const_spec((LANES, LANES)),
                  pl.BlockSpec((1, cols), lambda i, h: (0, h))],
        out_specs=spec,
        out_shape=jax.ShapeDtypeStruct(lq.shape, BF16),
        scratch_shapes=[pltpu.VMEM((per_step, 3, s, LANES), F32),
                        pltpu.VMEM((per_step, 3, s, LANES), F32),
                        pltpu.VMEM((per_step, npairs - 1, s, LANES), BF16),
                        pltpu.VMEM((per_step, npairs - 1, s, LANES), BF16),
                        pltpu.VMEM((per_step, npairs, 2, s, LANES), BF16),
                        f32_rows, f32_rows, f32_rows, f32_rows, f32_rows],
        compiler_params=_params(("parallel", "parallel")),
        name="dil_attn",
    )(bound, lq, lk, lv, rmat, gd)


def _mix_ffn_kernel(x_ref, mod_ref, oa_ref, ob_ref, wo_ref, g_ref, wu_ref, cw_ref, cb_ref, wd_ref,
                    xo_ref, tail_ref):
    d = x_ref.shape[-1]
    tm = x_ref.shape[1]
    d_ff = wd_ref.shape[0]
    nchunk = d_ff // FF_CHUNK
    halo = tail_ref.shape[2]
    wa = oa_ref.shape[-1]

    @pl.when(pl.program_id(1) == 0)
    def _():
        tail_ref[...] = jnp.zeros_like(tail_ref)

    mod = mod_ref[0]
    mixed = _dot(oa_ref[0], wo_ref[:wa, :])
    gw = ob_ref.shape[-1]
    for grp in range(ob_ref.shape[1]):
        mixed = mixed + _dot(ob_ref[0, grp], wo_ref[wa + grp * gw:wa + (grp + 1) * gw, :])
    x = x_ref[0] + mod[:, 2 * d:3 * d] * mixed
    h = _modulated_norm(x, g_ref[...], mod[:, 3 * d:4 * d], mod[:, 4 * d:5 * d]).astype(BF16)
    row = lax.broadcasted_iota(jnp.int32, (halo, FF_CHUNK), 0)

    def chunk(c):
        convs = []
        for part in range(2):
            cols = slice(part * d_ff + c * FF_CHUNK, part * d_ff + (c + 1) * FF_CHUNK)
            u = _dot(h, wu_ref[:, cols])
            prev = tail_ref[c, part]
            tail_ref[c, part] = u[tm - halo:, :]
            conv = cb_ref[:, cols] + u * cw_ref[CONV_WIDTH - 1:CONV_WIDTH, cols]
            for lag in range(1, CONV_WIDTH):
                shifted = pltpu.roll(u, lag, 0)
                head = jnp.where(row < lag, pltpu.roll(prev, lag, 0), shifted[:halo])
                shifted = jnp.concatenate([head, shifted[halo:]], axis=0)
                conv = conv + shifted * cw_ref[CONV_WIDTH - 1 - lag:CONV_WIDTH - lag, cols]
            convs.append(conv)
        half_gate, val = convs
        return ((half_gate + half_gate * jnp.tanh(half_gate)) * val).astype(BF16)

    y = None
    for c0 in range(0, nchunk, FF_DOWN_GROUP):
        c1 = min(c0 + FF_DOWN_GROUP, nchunk)
        act = jnp.concatenate([chunk(c) for c in range(c0, c1)], axis=1)
        part = _dot(act, wd_ref[c0 * FF_CHUNK:c1 * FF_CHUNK, :])
        y = part if y is None else y + part
    xo_ref[0] = x + mod[:, 5 * d:6 * d] * y


def _mix_ffn_call(x, mod_l, oa, ob, wo, g, wu, cw, cb, wd):
    b, s, d = x.shape
    tm = FFN_TILE
    tok = lambda w: pl.BlockSpec((1, tm, w), lambda i, j: (i, j, 0))
    nchunk = wd.shape[0] // FF_CHUNK
    return pl.pallas_call(
        _mix_ffn_kernel,
        grid=(b, s // tm),
        in_specs=[tok(d), pl.BlockSpec((1, 1, mod_l.shape[-1]), lambda i, j: (i, 0, 0)),
                  tok(oa.shape[-1]), pl.BlockSpec((1, ob.shape[1], tm, ob.shape[-1]), lambda i, j: (i, 0, j, 0)),
                  _const_spec(wo.shape, True), _const_spec((1, d)),
                  _const_spec(wu.shape, True), _const_spec(cw.shape), _const_spec(cb.shape),
                  _const_spec(wd.shape, True)],
        out_specs=tok(d),
        out_shape=jax.ShapeDtypeStruct(x.shape, x.dtype),
        scratch_shapes=[pltpu.VMEM((nchunk, 2, SUBLANES, FF_CHUNK), F32)],
        compiler_params=_params(("parallel", "arbitrary")),
        name="mix_ffn",
    )(x, mod_l, oa, ob, wo, g, wu, cw, cb, wd)


def _group_mean_matrix(n):
    idx = np.arange(n) // HEAD_DIM
    return jnp.asarray((idx[:, None] == idx[None, :]).astype(np.float32) / HEAD_DIM, dtype=BF16)


def _logit_bound(g_q, g_k):
    return (HEAD_DIM * QK_SCALE * jnp.max(jnp.abs(g_q)) * jnp.max(jnp.abs(g_k))).reshape(1).astype(F32)


def kernel(x, c, positions, g_mix, g_ffn, w_ada, b_ada, w_in, w_out, diff_q_g, diff_k_g, lam_q1, lam_k1, lam_q2, lam_k2, diff_subln_g, dil_q_g, dil_k_g, dil_out_g, w_up, conv_w, conv_b, w_down):
    depth, d, _ = w_in.shape
    b = x.shape[0]
    d_ff = w_down.shape[1]

    lam_init = np.array([0.8 - 0.6 * math.exp(-0.3 * l) for l in range(depth)], np.float32)
    lam_init_tile = jnp.asarray(np.broadcast_to(lam_init[:, None, None], (depth, 1, LANES)))
    mod, lam = _mod_call(c, w_ada, b_ada, lam_q1, lam_k1, lam_q2, lam_k2, lam_init_tile)
    cos_t, sin_t = _rope_call(positions)
    rmat_qk = _group_mean_matrix(MXU_DIM)
    rmat_out = _group_mean_matrix(LANES)
    gate_half = jnp.concatenate([jnp.full((1, d_ff), 0.5, F32), jnp.ones((1, d_ff), F32)], axis=1)
    reps = DIFF_WIDTH // HEAD_DIM
    ones = jnp.ones((DIFF_WIDTH,), F32)

    for l in range(depth):
        gqk = jnp.concatenate([jnp.tile(diff_q_g[l], reps) * QK_SCALE, jnp.tile(diff_k_g[l], reps), ones,
                               jnp.tile(dil_q_g[l], reps) * QK_SCALE, jnp.tile(dil_k_g[l], reps), ones]).reshape(1, -1)
        mod_l = mod[l].reshape(b, 1, -1)
        dq, dk, dv, lq, lk, lv = _in_proj_call(x, mod_l, g_mix[l].reshape(1, d), w_in[l].astype(BF16), rmat_qk, gqk,
                                               cos_t, sin_t)
        g_sub = (diff_subln_g[l] * (1.0 - float(lam_init[l]))).reshape(1, DIFF_V_DIM)
        oa = _diff_attn_call(_logit_bound(diff_q_g[l], diff_k_g[l]), dq, dk, dv, lam[l], g_sub)
        gd = jnp.tile(dil_out_g[l], N_DIL_HEADS).reshape(1, DIL_WIDTH)
        ob = _dil_attn_call(_logit_bound(dil_q_g[l], dil_k_g[l]), lq, lk, lv, rmat_out, gd)
        x = _mix_ffn_call(x, mod_l, oa, ob, w_out[l].astype(BF16), g_ffn[l].reshape(1, d), w_up[l].astype(BF16),
                          conv_w[l] * gate_half, conv_b[l].reshape(1, -1) * gate_half, w_down[l].astype(BF16))
    return x
```

```python
import math

import jax
import jax.numpy as jnp
import numpy as np
from jax import lax
from jax.experimental import pallas as pl
from jax.experimental.pallas import tpu as pltpu

HEAD_DIM = 64
N_DIFF_HEADS = 4
DIFF_V_DIM = 2 * HEAD_DIM
DIFF_WIDTH = N_DIFF_HEADS * DIFF_V_DIM
N_DIL_HEADS = 8
DIL_WIDTH = N_DIL_HEADS * HEAD_DIM
DILATED_PAIRS = ((128, 1), (512, 4), (2048, 16))
ROPE_THETA = 500000.0
ROPE_DIM = HEAD_DIM // 4
ROPE_HALF = ROPE_DIM // 2
CONV_WIDTH = 3
EPS = 1e-6
NEG_INF = -1e30
LOG2E = 1.4426950408889634
QK_SCALE = HEAD_DIM ** -0.5 * LOG2E

LANES = 128
SUBLANES = 8
MXU_DIM = 256
DIFF_Q_BLOCK = 256
FFN_TILE = 512
IN_PROJ_TILE = 1024
DIFF_HEADS_PER_STEP = 4
DIL_BLOCK = 128
DIL_LANE_BLOCKS_PER_STEP = 2
FF_CHUNK = 256
FF_DOWN_GROUP = 11
SAFE_LOGIT = 64.0
VMEM_LIMIT = 56 * 1024 * 1024

F32 = jnp.float32
BF16 = jnp.bfloat16


def _dot(a, b):
    return jnp.dot(a, b, preferred_element_type=F32)


def _dot_nt(a, b):
    return lax.dot_general(a, b, (((1,), (1,)), ((), ())), preferred_element_type=F32)


def _split_bf16(x):
    hi = x.astype(BF16)
    lo = (x - hi.astype(F32)).astype(BF16)
    return hi, lo


def _params(sem, vmem=VMEM_LIMIT, fuse_inputs=None):
    return pltpu.CompilerParams(dimension_semantics=sem, vmem_limit_bytes=vmem, allow_input_fusion=fuse_inputs)


def _const_spec(shape, single_buffer=False):
    mode = pl.Buffered(1) if single_buffer else None
    return pl.BlockSpec(shape, lambda *_: (0,) * len(shape), pipeline_mode=mode)


_SMEM_SPEC = pl.BlockSpec(memory_space=pltpu.SMEM)


def _mod_kernel(c_ref, w_ref, b_ref, q1_ref, k1_ref, q2_ref, k2_ref, li_ref, mod_ref, lam_ref):
    c = c_ref[...]
    cond = c * (1.0 / (1.0 + jnp.exp(-c)))
    ch, cl = _split_bf16(cond)
    wh, wl = _split_bf16(w_ref[0])
    mod_ref[0] = _dot(ch, wh) + (_dot(ch, wl) + _dot(cl, wh)) + b_ref[0]
    s1 = jnp.sum(q1_ref[0] * k1_ref[0], axis=-1, keepdims=True)
    s2 = jnp.sum(q2_ref[0] * k2_ref[0], axis=-1, keepdims=True)
    lam_ref[0] = (jnp.exp(s1) - jnp.exp(s2)) + li_ref[0]


def _mod_call(c, w_ada, b_ada, lam_q1, lam_k1, lam_q2, lam_k2, lam_init):
    depth, d, d6 = w_ada.shape
    b = c.shape[0]
    nj = d6 // d
    vec = lambda a: a.reshape(depth, 1, HEAD_DIM)
    vspec = pl.BlockSpec((1, 1, HEAD_DIM), lambda l, j: (l, 0, 0))
    return pl.pallas_call(
        _mod_kernel,
        grid=(depth, nj),
        in_specs=[
            pl.BlockSpec((b, d), lambda l, j: (0, 0)),
            pl.BlockSpec((1, d, d), lambda l, j: (l, 0, j)),
            pl.BlockSpec((1, 1, d), lambda l, j: (l, 0, j)),
            vspec, vspec, vspec, vspec,
            pl.BlockSpec((1, 1, LANES), lambda l, j: (l, 0, 0)),
        ],
        out_specs=[
            pl.BlockSpec((1, b, d), lambda l, j: (l, 0, j)),
            pl.BlockSpec((1, 1, LANES), lambda l, j: (l, 0, 0)),
        ],
        out_shape=[
            jax.ShapeDtypeStruct((depth, b, d6), F32),
            jax.ShapeDtypeStruct((depth, 1, LANES), F32),
        ],
        compiler_params=_params(("arbitrary", "arbitrary")),
        name="adaln_mod",
    )(c, w_ada, b_ada.reshape(depth, 1, d6), vec(lam_q1), vec(lam_k1), vec(lam_q2), vec(lam_k2), lam_init)


def _rope_kernel(pos_ref, invf_ref, c_ref, s_ref):
    ang = pos_ref[0].astype(F32) * invf_ref[...]
    c_ref[0] = jnp.cos(ang)
    s_ref[0] = jnp.sin(ang)


def _rope_call(positions):
    b, s = positions.shape
    per_row = LANES // ROPE_HALF
    rows = s // per_row
    inv_freq = ROPE_THETA ** (-jnp.arange(0, ROPE_DIM, 2, dtype=F32) / ROPE_DIM)
    invf = jnp.tile(inv_freq, per_row).reshape(1, LANES)
    pos = jnp.repeat(positions.reshape(b, rows, per_row), ROPE_HALF, axis=-1)
    spec = pl.BlockSpec((1, rows, LANES), lambda i: (i, 0, 0))
    out = jax.ShapeDtypeStruct((b, rows, LANES), F32)
    cos, sin = pl.pallas_call(
        _rope_kernel,
        grid=(b,),
        in_specs=[spec, _const_spec((1, LANES))],
        out_specs=[spec, spec],
        out_shape=[out, out],
        compiler_params=_params(("parallel",)),
        name="rope_tables",
    )(pos, invf)
    lane_tile = lambda t: jnp.tile(t.reshape(b, s, ROPE_HALF), (1, 1, per_row))
    return lane_tile(cos), lane_tile(sin)


def _modulated_norm(x, g, shift, scale):
    ms = jnp.mean(x * x, axis=-1, keepdims=True)
    return (x * lax.rsqrt(ms + EPS) * g) * (1.0 + scale) + shift


def _group_mean_sq(y, r_ref):
    return _dot((y * y).astype(BF16), r_ref[...])


def _in_proj_kernel(x_ref, mod_ref, g_ref, w_ref, r_ref, gqk_ref, cos_ref, sin_ref,
                    dq_ref, dk_ref, dv_ref, lq_ref, lk_ref, lv_ref):
    d = x_ref.shape[-1]
    mod = mod_ref[0]
    h = _modulated_norm(x_ref[0], g_ref[...], mod[:, 0:d], mod[:, d:2 * d]).astype(BF16)
    dim = lax.broadcasted_iota(jnp.int32, cos_ref.shape[1:], 1) % HEAD_DIM
    first_half = dim < ROPE_HALF
    rope_c = jnp.where(dim < ROPE_DIM, cos_ref[0], 1.0)
    rope_s = jnp.where(first_half, -sin_ref[0], jnp.where(dim < ROPE_DIM, sin_ref[0], 0.0))
    width = dq_ref.shape[-1]
    for sec, o_ref in enumerate((dq_ref, dk_ref, dv_ref, lq_ref, lk_ref, lv_ref)):
        p = _dot(h, w_ref[:, sec * width:(sec + 1) * width])
        if o_ref is dv_ref or o_ref is lv_ref:
            o_ref[0] = p.astype(BF16)
            continue
        for c in range(width // MXU_DIM):
            col = sec * width + c * MXU_DIM
            xc = p[:, c * MXU_DIM:(c + 1) * MXU_DIM]
            y = xc * lax.rsqrt(_group_mean_sq(xc, r_ref) + EPS) * gqk_ref[:, col:col + MXU_DIM]
            for t in range(MXU_DIM // LANES):
                yt = y[:, t * LANES:(t + 1) * LANES]
                partner = jnp.where(first_half, pltpu.roll(yt, LANES - ROPE_HALF, 1), pltpu.roll(yt, ROPE_HALF, 1))
                out_col = c * MXU_DIM + t * LANES
                o_ref[0, :, out_col:out_col + LANES] = (yt * rope_c + partner * rope_s).astype(BF16)


def _in_proj_call(x, mod_l, g, w, rmat, gqk, cos_t, sin_t):
    b, s, d = x.shape
    tm = IN_PROJ_TILE
    tok = lambda w: pl.BlockSpec((1, tm, w), lambda i, j: (i, j, 0))
    width = w.shape[1] // 6
    out = jax.ShapeDtypeStruct((b, s, width), BF16)
    return pl.pallas_call(
        _in_proj_kernel,
        grid=(b, s // tm),
        in_specs=[
            tok(d),
            pl.BlockSpec((1, 1, mod_l.shape[-1]), lambda i, j: (i, 0, 0)),
            _const_spec((1, d)),
            _const_spec(w.shape, True), _const_spec(rmat.shape), _const_spec(gqk.shape),
            tok(LANES), tok(LANES),
        ],
        out_specs=[tok(width)] * 6,
        out_shape=[out] * 6,
        compiler_params=_params(("parallel", "parallel"),
                                fuse_inputs=[False, False, False, True, False, False, True, True]),
        name="in_proj",
    )(x, mod_l, g, w, rmat, gqk, cos_t, sin_t)


def _diff_attn_kernel(bound_ref, q_ref, k_ref, v_ref, lam_ref, g_ref, o_ref):
    tq = DIFF_Q_BLOCK
    seq = k_ref.shape[1]
    nq = seq // tq
    heads = q_ref.shape[2] // LANES
    step_id = pl.program_id(2)
    lanes = lambda hh: slice(hh * LANES, (hh + 1) * LANES)

    def stacked_q(hh, q0):
        q = q_ref[0, pl.ds(q0, tq), lanes(hh)]
        lane = lax.broadcasted_iota(jnp.int32, q.shape, 1)
        zero = jnp.zeros_like(q)
        return jnp.concatenate([jnp.where(lane < HEAD_DIM, q, zero), jnp.where(lane >= HEAD_DIM, q, zero)], axis=0)

    def causal(s, k0):
        qry = lax.broadcasted_iota(jnp.int32, s.shape, 0) % tq
        key = lax.broadcasted_iota(jnp.int32, s.shape, 1) + k0
        return jnp.where(key <= qry, s, NEG_INF)

    def v_ext(hh, k0, n):
        return jnp.concatenate([v_ref[0, pl.ds(k0, n), lanes(hh)], jnp.ones((n, LANES), BF16)], axis=1)

    def finish(hh, q0, acc):
        o1 = acc[:tq, :LANES] / acc[:tq, LANES:]
        o2 = acc[tq:, :LANES] / acc[tq:, LANES:]
        o = o1 - lam_ref[0:1, 0:1] * o2
        ms = jnp.mean(o * o, axis=-1, keepdims=True)
        o_ref[0, pl.ds(q0, tq), lanes(hh)] = (o * lax.rsqrt(ms + EPS) * g_ref[...]).astype(o_ref.dtype)

    def unshifted(blk):
        kv = (blk + 1) * tq
        for hh in range(heads):
            s = causal(_dot_nt(stacked_q(hh, blk * tq), k_ref[0, :kv, lanes(hh)]), -blk * tq)
            finish(hh, blk * tq, _dot(jnp.exp2(s).astype(BF16), v_ext(hh, 0, kv)))

    def shifted(blk):
        q0 = pl.multiple_of(blk * tq, tq)
        for hh in range(heads):
            qb = stacked_q(hh, q0)

            def step(j, carry, diagonal, hh=hh, qb=qb):
                m, acc = carry
                k0 = pl.multiple_of(j * tq, tq)
                s = _dot_nt(qb, k_ref[0, pl.ds(k0, tq), lanes(hh)])
                if diagonal:
                    s = causal(s, 0)
                m_new = jnp.maximum(m, jnp.max(s, axis=-1, keepdims=True))
                acc = jnp.exp2(m - m_new) * acc + _dot(jnp.exp2(s - m_new).astype(BF16), v_ext(hh, k0, tq))
                return m_new, acc

            carry = (jnp.full((2 * tq, 1), NEG_INF, F32), jnp.zeros((2 * tq, 2 * LANES), F32))
            carry = lax.fori_loop(0, blk, lambda j, c, step=step: step(j, c, False), carry)
            finish(hh, q0, step(blk, carry, True)[1])

    safe = bound_ref[0] <= SAFE_LOGIT
    for pair in range(nq // 2):
        @pl.when(jnp.logical_and(step_id == pair, safe))
        def _(pair=pair):
            unshifted(pair)
            unshifted(nq - 1 - pair)

    @pl.when(jnp.logical_not(safe))
    def _():
        def one_block(which, _):
            shifted(jnp.where(which == 0, step_id, nq - 1 - step_id))
            return 0
        lax.fori_loop(0, 2, one_block, 0)


def _diff_attn_call(bound, dq, dk, dv, lam_l, g_sub):
    b, s, w = dq.shape
    cols = DIFF_HEADS_PER_STEP * LANES
    rows = pl.BlockSpec((1, s, cols), lambda i, h, j: (i, 0, h))
    return pl.pallas_call(
        _diff_attn_kernel,
        grid=(b, w // cols, s // DIFF_Q_BLOCK // 2),
        in_specs=[_SMEM_SPEC, rows, rows, rows, _const_spec((1, LANES)), _const_spec((1, LANES))],
        out_specs=rows,
        out_shape=jax.ShapeDtypeStruct((b, s, w), BF16),
        compiler_params=_params(("parallel", "parallel", "arbitrary")),
        name="diff_attn",
    )(bound, dq, dk, dv, lam_l, g_sub)


def _rows(start, size, stride):
    return pl.ds(start, size, stride=stride) if stride > 1 else pl.ds(start, size)


def _dil_lane_block(q_ref, k_ref, v_ref, r_ref, g_ref, o_ref,
                    stage, stage_mid, qp, kp, vm, num, den, mx, num_mid, den_mid):
    seq = q_ref.shape[1]
    blk = DIL_BLOCK
    head0 = lax.broadcasted_iota(jnp.int32, (blk, LANES), 1) < HEAD_DIM
    strides = tuple(stride for _, stride in DILATED_PAIRS)
    mid = strides[1]
    mid_len = seq // mid

    def mid_rows(r, n_rows):
        return pl.ds((r % mid) * mid_len + r // mid, n_rows, stride=mid)

    def stage_rows(di, dst_rows, q, k, v):
        lane0 = lax.broadcasted_iota(jnp.int32, v.shape, 1) < HEAD_DIM
        one = jnp.ones_like(v)
        if di > 0:
            qp[di - 1, dst_rows, :] = q
            kp[di - 1, dst_rows, :] = k
        vm[di, 0, dst_rows, :] = jnp.where(lane0, v, one)
        vm[di, 1, dst_rows, :] = jnp.where(lane0, one, v)

    def stage_all():
        stage_rows(0, slice(None), None, None, v_ref[0])
        for j, ref in enumerate((q_ref, k_ref, v_ref)):
            stage[j] = ref[0].astype(F32)
            for r in range(mid):
                stage_mid[j, r * mid_len:(r + 1) * mid_len, :] = stage[j, pl.ds(r, mid_len, stride=mid), :]
        stage_rows(1, slice(None), *(stage_mid[j].astype(BF16) for j in range(3)))
        last_len = seq // strides[2]
        for r in range(strides[2]):
            stage_rows(2, slice(r * last_len, (r + 1) * last_len),
                       *(stage_mid[j, mid_rows(r, last_len), :].astype(BF16) for j in range(3)))

    def q_rows(di, rows):
        return q_ref[0, rows, :] if di == 0 else qp[di - 1, rows, :]

    def k_rows(di, rows):
        return k_ref[0, rows, :] if di == 0 else kp[di - 1, rows, :]

    def window(ctx):
        qq = lax.broadcasted_iota(jnp.int32, (2 * blk, ctx), 0) % blk
        kk = lax.broadcasted_iota(jnp.int32, (2 * blk, ctx), 1)
        return (kk <= qq) if ctx == blk else ((kk >= qq) & (kk <= qq + blk))

    def block(di, r, n, mode):
        stride = strides[di]
        first = di == 0
        q0 = r * (seq // stride) + n * blk
        k0 = q0 - blk if n > 0 else q0
        if not isinstance(r, int):
            q0, k0 = pl.multiple_of(q0, blk), pl.multiple_of(k0, blk)
        ctx = 2 * blk if n > 0 else blk
        out_rows = _rows(r + n * blk * stride, blk, stride)
        q = q_rows(di, pl.ds(q0, blk))
        zero = jnp.zeros_like(q)
        qb = jnp.concatenate([jnp.where(head0, q, zero), jnp.where(head0, zero, q)], axis=0)
        s = _dot_nt(qb, k_rows(di, pl.ds(k0, ctx)))
        s = jnp.where(window(ctx), s, NEG_INF)
        if mode == "max":
            m = jnp.max(s, axis=-1, keepdims=True)
            m = jnp.where(head0, m[:blk], m[blk:])
            mx[out_rows, :] = m if first else jnp.maximum(mx[out_rows, :], m)
            return
        if mode == "shifted":
            m = mx[out_rows, :]
            s = s - jnp.concatenate([m[:, 0:1], m[:, HEAD_DIM:HEAD_DIM + 1]], axis=0)
        p = jnp.exp2(s).astype(BF16)
        out0 = _dot(p[:blk], vm[di, 0, pl.ds(k0, ctx), :])
        out1 = _dot(p[blk:], vm[di, 1, pl.ds(k0, ctx), :])
        n_blk = jnp.where(head0, out0, out1)
        d_blk = jnp.where(head0, out1, out0)
        if di == 0:
            num[out_rows, :] = n_blk
            den[out_rows, :] = d_blk
        elif di == 2:
            num_mid[mid_rows(r, blk), :] = n_blk
            den_mid[mid_rows(r, blk), :] = d_blk
        else:
            num[out_rows, :] = num[out_rows, :] + (n_blk + num_mid[pl.ds(q0, blk), :])
            den[out_rows, :] = den[out_rows, :] + (d_blk + den_mid[pl.ds(q0, blk), :])

    def sweep(mode, straight_line):
        for di in (0, 2, 1):
            stride = strides[di]
            nb = seq // stride // blk
            if straight_line or stride == 1:
                for r in range(stride):
                    for n in range(nb):
                        block(di, r, n, mode)
            else:
                def subsequence(r, _, di=di, nb=nb):
                    for n in range(nb):
                        block(di, r, n, mode)
                    return 0
                lax.fori_loop(0, stride, subsequence, 0)

    def finish():
        o = num[...] / pltpu.roll(den[...], HEAD_DIM, 1)
        ms = _dot((o * o).astype(BF16), r_ref[...])
        o_ref[0] = (o * lax.rsqrt(ms + EPS) * g_ref[...]).astype(o_ref.dtype)

    return stage_all, sweep, finish


def _dil_attn_kernel(bound_ref, q_ref, k_ref, v_ref, r_ref, g_ref, o_ref, *scratch):
    parts = []
    for hp in range(q_ref.shape[2] // LANES):
        cols = pl.ds(hp * LANES, LANES)
        parts.append(_dil_lane_block(q_ref.at[:, :, cols], k_ref.at[:, :, cols], v_ref.at[:, :, cols], r_ref,
                                     g_ref.at[:, cols], o_ref.at[:, :, cols], *(ref.at[hp] for ref in scratch)))
    safe = bound_ref[0] <= SAFE_LOGIT

    @pl.when(safe)
    def _():
        for stage_all, _, _ in parts:
            stage_all()
        for _, sweep, _ in parts:
            sweep("plain", True)
        for _, _, finish in parts:
            finish()

    @pl.when(jnp.logical_not(safe))
    def _():
        for stage_all, sweep, finish in parts:
            stage_all()
            sweep("max", False)
            sweep("shifted", False)
            finish()


def _dil_attn_call(bound, lq, lk, lv, rmat, gd):
    b, s, w = lq.shape
    per_step = DIL_LANE_BLOCKS_PER_STEP
    cols = per_step * LANES
    spec = pl.BlockSpec((1, s, cols), lambda i, h: (i, 0, h))
    f32_rows = pltpu.VMEM((per_step, s, LANES), F32)
    npairs = len(DILATED_PAIRS)
    strides = tuple(stride for _, stride in DILATED_PAIRS)
    assert strides == (1, strides[1], strides[1] ** 2) and s // strides[2] == DIL_BLOCK
    assert all(window // stride == DIL_BLOCK for window, stride in DILATED_PAIRS)
    return pl.pallas_call(
        _dil_attn_kernel,
        grid=(b, w // cols),
        in_specs=[_SMEM_SPEC, spec, spec, spec, _const_spec((LANES, LANES)),
                  pl.BlockSpec((1, cols), lambda i, h: (0, h))],
        out_specs=spec,
        out_shape=jax.ShapeDtypeStruct((b, s, w), BF16),
        scratch_shapes=[pltpu.VMEM((per_step, 3, s, LANES), F32),
                        pltpu.VMEM((per_step, 3, s, LANES), F32),
                        pltpu.VMEM((per_step, npairs - 1, s, LANES), BF16),
                        pltpu.VMEM((per_step, npairs - 1, s, LANES), BF16),
                        pltpu.VMEM((per_step, npairs, 2, s, LANES), BF16),
                        f32_rows, f32_rows, f32_rows, f32_rows, f32_rows],
        compiler_params=_params(("parallel", "parallel")),
        name="dil_attn",
    )(bound, lq, lk, lv, rmat, gd)


def _mix_ffn_kernel(x_ref, mod_ref, oa_ref, ob_ref, wo_ref, g_ref, wu_ref, cw_ref, cb_ref, wd_ref,
                    xo_ref, tail_ref):
    d = x_ref.shape[-1]
    tm = x_ref.shape[1]
    d_ff = wd_ref.shape[0]
    nchunk = d_ff // FF_CHUNK
    halo = tail_ref.shape[2]
    wa = oa_ref.shape[-1]

    @pl.when(pl.program_id(1) == 0)
    def _():
        tail_ref[...] = jnp.zeros_like(tail_ref)

    mod = mod_ref[0]
    mixed = _dot(oa_ref[0], wo_ref[:wa, :]) + _dot(ob_ref[0], wo_ref[wa:, :])
    x = x_ref[0] + mod[:, 2 * d:3 * d] * mixed
    h = _modulated_norm(x, g_ref[...], mod[:, 3 * d:4 * d], mod[:, 4 * d:5 * d]).astype(BF16)
    row = lax.broadcasted_iota(jnp.int32, (halo, FF_CHUNK), 0)

    def chunk(c):
        convs = []
        for part in range(2):
            cols = slice(part * d_ff + c * FF_CHUNK, part * d_ff + (c + 1) * FF_CHUNK)
            u = _dot(h, wu_ref[:, cols])
            prev = tail_ref[c, part]
            tail_ref[c, part] = u[tm - halo:, :]
            conv = cb_ref[:, cols] + u * cw_ref[CONV_WIDTH - 1:CONV_WIDTH, cols]
            for lag in range(1, CONV_WIDTH):
                shifted = pltpu.roll(u, lag, 0)
                head = jnp.where(row < lag, pltpu.roll(prev, lag, 0), shifted[:halo])
                shifted = jnp.concatenate([head, shifted[halo:]], axis=0)
                conv = conv + shifted * cw_ref[CONV_WIDTH - 1 - lag:CONV_WIDTH - lag, cols]
            convs.append(conv)
        half_gate, val = convs
        return ((half_gate + half_gate * jnp.tanh(half_gate)) * val).astype(BF16)

    y = None
    for c0 in range(0, nchunk, FF_DOWN_GROUP):
        c1 = min(c0 + FF_DOWN_GROUP, nchunk)
        act = jnp.concatenate([chunk(c) for c in range(c0, c1)], axis=1)
        part = _dot(act, wd_ref[c0 * FF_CHUNK:c1 * FF_CHUNK, :])
        y = part if y is None else y + part
    xo_ref[0] = x + mod[:, 5 * d:6 * d] * y


def _mix_ffn_call(x, mod_l, oa, ob, wo, g, wu, cw, cb, wd):
    b, s, d = x.shape
    tm = FFN_TILE
    tok = lambda w: pl.BlockSpec((1, tm, w), lambda i, j: (i, j, 0))
    nchunk = wd.shape[0] // FF_CHUNK
    return pl.pallas_call(
        _mix_ffn_kernel,
        grid=(b, s // tm),
        in_specs=[tok(d), pl.BlockSpec((1, 1, mod_l.shape[-1]), lambda i, j: (i, 0, 0)),
                  tok(oa.shape[-1]), tok(ob.shape[-1]), _const_spec(wo.shape, True), _const_spec((1, d)),
                  _const_spec(wu.shape, True), _const_spec(cw.shape), _const_spec(cb.shape),
                  _const_spec(wd.shape, True)],
        out_specs=tok(d),
        out_shape=jax.ShapeDtypeStruct(x.shape, x.dtype),
        scratch_shapes=[pltpu.VMEM((nchunk, 2, SUBLANES, FF_CHUNK), F32)],
        compiler_params=_params(("parallel", "arbitrary")),
        name="mix_ffn",
    )(x, mod_l, oa, ob, wo, g, wu, cw, cb, wd)


def _group_mean_matrix(n):
    idx = np.arange(n) // HEAD_DIM
    return jnp.asarray((idx[:, None] == idx[None, :]).astype(np.float32) / HEAD_DIM, dtype=BF16)


def _logit_bound(g_q, g_k):
    return (HEAD_DIM * QK_SCALE * jnp.max(jnp.abs(g_q)) * jnp.max(jnp.abs(g_k))).reshape(1).astype(F32)


def kernel(x, c, positions, g_mix, g_ffn, w_ada, b_ada, w_in, w_out, diff_q_g, diff_k_g, lam_q1, lam_k1, lam_q2, lam_k2, diff_subln_g, dil_q_g, dil_k_g, dil_out_g, w_up, conv_w, conv_b, w_down):
    depth, d, _ = w_in.shape
    b = x.shape[0]
    d_ff = w_down.shape[1]

    lam_init = np.array([0.8 - 0.6 * math.exp(-0.3 * l) for l in range(depth)], np.float32)
    lam_init_tile = jnp.asarray(np.broadcast_to(lam_init[:, None, None], (depth, 1, LANES)))
    mod, lam = _mod_call(c, w_ada, b_ada, lam_q1, lam_k1, lam_q2, lam_k2, lam_init_tile)
    cos_t, sin_t = _rope_call(positions)
    rmat_qk = _group_mean_matrix(MXU_DIM)
    rmat_out = _group_mean_matrix(LANES)
    gate_half = jnp.concatenate([jnp.full((1, d_ff), 0.5, F32), jnp.ones((1, d_ff), F32)], axis=1)
    reps = DIFF_WIDTH // HEAD_DIM
    ones = jnp.ones((DIFF_WIDTH,), F32)

    for l in range(depth):
        gqk = jnp.concatenate([jnp.tile(diff_q_g[l], reps) * QK_SCALE, jnp.tile(diff_k_g[l], reps), ones,
                               jnp.tile(dil_q_g[l], reps) * QK_SCALE, jnp.tile(dil_k_g[l], reps), ones]).reshape(1, -1)
        mod_l = mod[l].reshape(b, 1, -1)
        dq, dk, dv, lq, lk, lv = _in_proj_call(x, mod_l, g_mix[l].reshape(1, d), w_in[l].astype(BF16), rmat_qk, gqk,
                                               cos_t, sin_t)
        g_sub = (diff_subln_g[l] * (1.0 - float(lam_init[l]))).reshape(1, DIFF_V_DIM)
        oa = _diff_attn_call(_logit_bound(diff_q_g[l], diff_k_g[l]), dq, dk, dv, lam[l], g_sub)
        gd = jnp.tile(dil_out_g[l], N_DIL_HEADS).reshape(1, DIL_WIDTH)
        ob = _dil_attn_call(_logit_bound(dil_q_g[l], dil_k_g[l]), lq, lk, lv, rmat_out, gd)
        x = _mix_ffn_call(x, mod_l, oa, ob, w_out[l].astype(BF16), g_ffn[l].reshape(1, d), w_up[l].astype(BF16),
                          conv_w[l] * gate_half, conv_b[l].reshape(1, -1) * gate_half, w_down[l].astype(BF16))
    return x
```
